```python
import math
import jax, jax.numpy as jnp
from jax import lax
import numpy as np

D_MODEL = 1024
BATCH = 16
SEQ = 2048
DEPTH = 2

N_MIXERS = 2
MEM_LEN = 256
MEM_HEADS = 4
MEM_HEAD_DIM = D_MODEL // 16
MEM_WIDTH = MEM_HEADS * MEM_HEAD_DIM
RET_HEAD_DIM = D_MODEL // 8
RET_HEADS = (D_MODEL - MEM_WIDTH) // RET_HEAD_DIM
RET_WIDTH = RET_HEADS * RET_HEAD_DIM
RET_CHUNK = 128
ROPE_BASE = 10000.0
FOX_HEAD_DIM = D_MODEL // 16
FOX_HEADS = (D_MODEL - MEM_WIDTH) // FOX_HEAD_DIM
FOX_WIDTH = FOX_HEADS * FOX_HEAD_DIM
FOX_BLOCK = 128
D_FF = 4 * D_MODEL
EPS = 1e-6
N_RET = (DEPTH + N_MIXERS - 1) // N_MIXERS
N_FOX = DEPTH // N_MIXERS
RET_IN = 4 * RET_WIDTH + MEM_WIDTH
FOX_IN = 3 * FOX_WIDTH + FOX_HEADS + MEM_WIDTH

kernel_name = "hybrid_retention_fox_memory_block"


def rmsnorm(x, g):
    xf = x.astype(jnp.float32)
    y = xf * lax.rsqrt(jnp.mean(xf * xf, axis=-1, keepdims=True) + EPS)
    return (y * g.astype(jnp.float32)).astype(x.dtype)


def rotary(x, positions):
    d = x.shape[-1]
    inv_freq = ROPE_BASE ** (-jnp.arange(0, d, 2, dtype=jnp.float32) / d)
    ang = positions.astype(jnp.float32)[..., None] * inv_freq
    cos = jnp.cos(ang)[:, :, None, :]
    sin = jnp.sin(ang)[:, :, None, :]
    xf = x.astype(jnp.float32)
    x1, x2 = xf[..., : d // 2], xf[..., d // 2:]
    return jnp.concatenate([x1 * cos - x2 * sin, x1 * sin + x2 * cos], axis=-1)


def retention_chunkwise(q, k, v):
    B, S, H, d = q.shape
    C = RET_CHUNK
    n = S // C
    log_gamma = jnp.log(1.0 - 2.0 ** (-5.0 - jnp.arange(H, dtype=jnp.float32)))
    idx = jnp.arange(C, dtype=jnp.float32)
    rel = idx[:, None] - idx[None, :]
    decay_mask = jnp.where(rel[None] >= 0, jnp.exp(jnp.maximum(rel, 0.0)[None] * log_gamma[:, None, None]), 0.0)
    q_decay = jnp.exp((idx[None, :] + 1.0) * log_gamma[:, None])
    k_decay = jnp.exp((C - 1.0 - idx[None, :]) * log_gamma[:, None])
    chunk_decay = jnp.exp(C * log_gamma)
    k = k * (d ** -0.5)

    def to_chunks(t):
        return t.reshape(B, n, C, H, t.shape[-1]).transpose(1, 0, 3, 2, 4)

    def step(state, inp):
        qc, kc, vc = inp
        inner = jnp.einsum('bhid,bhjd->bhij', qc, kc) * decay_mask[None]
        o_inner = jnp.einsum('bhij,bhje->bhie', inner, vc)
        o_cross = jnp.einsum('bhid,bhde->bhie', qc * q_decay[None, :, :, None], state)
        state = state * chunk_decay[None, :, None, None] + jnp.einsum('bhjd,bhje->bhde', kc * k_decay[None, :, :, None], vc)
        return state, o_inner + o_cross

    state0 = jnp.zeros((B, H, d, v.shape[-1]), jnp.float32)
    _, out = lax.scan(step, state0, (to_chunks(q), to_chunks(k), to_chunks(v)))
    return out.transpose(1, 0, 3, 2, 4).reshape(B, S, H, v.shape[-1])


def forgetting_attention(q, k, v, log_f):
    B, S, H, d = q.shape
    nb = S // FOX_BLOCK
    c = jnp.cumsum(log_f, axis=1).transpose(0, 2, 1)
    qh = q.transpose(0, 2, 1, 3)
    kh = k.transpose(0, 2, 1, 3)
    vh = v.transpose(0, 2, 1, 3)
    q_blocks = qh.reshape(B, H, nb, FOX_BLOCK, d).transpose(2, 0, 1, 3, 4)
    c_blocks = c.reshape(B, H, nb, FOX_BLOCK).transpose(2, 0, 1, 3)
    starts = jnp.arange(nb, dtype=jnp.int32) * FOX_BLOCK
    key_pos = jnp.arange(S, dtype=jnp.int32)
    scale = d ** -0.5

    def one_block(inp):
        qb, cb, start = inp
        s = jnp.einsum('bhqd,bhkd->bhqk', qb, kh).astype(jnp.float32) * scale
        s = s + cb[..., None] - c[:, :, None, :]
        q_pos = start + jnp.arange(FOX_BLOCK, dtype=jnp.int32)
        causal = key_pos[None, :] <= q_pos[:, None]
        s = jnp.where(causal[None, None], s, -jnp.inf)
        p = jax.nn.softmax(s, axis=-1)
        return jnp.einsum('bhqk,bhkd->bhqd', p.astype(vh.dtype), vh)

    out = lax.map(one_block, (q_blocks, c_blocks, starts))
    return out.transpose(1, 0, 3, 2, 4).reshape(B, S, H, d)


def memory_attention(mq, mem, g_mem, w_mem_kv):
    B, M, _ = mem.shape
    kv = rmsnorm(mem, g_mem) @ w_mem_kv
    mk, mv = jnp.split(kv, 2, axis=-1)
    mk = mk.reshape(B, M, MEM_HEADS, MEM_HEAD_DIM)
    mv = mv.reshape(B, M, MEM_HEADS, MEM_HEAD_DIM)
    s = jnp.einsum('bshd,bmhd->bhsm', mq, mk).astype(jnp.float32) * (MEM_HEAD_DIM ** -0.5)
    p = jax.nn.softmax(s, axis=-1)
    return jnp.einsum('bhsm,bmhd->bshd', p.astype(mv.dtype), mv)


def setup_inputs(seed: int = 0) -> dict:
    key = jax.random.key(seed)
    ks = jax.random.split(key, 16)
    f32 = jnp.float32

    def w(k, shape, fan_in):
        return jax.random.normal(k, shape, f32) * (fan_in ** -0.5)

    def gain(k, shape):
        return 1.0 + 0.02 * jax.random.normal(k, shape, f32)

    x = jax.random.normal(ks[0], (BATCH, SEQ, D_MODEL), f32)
    mem = jax.random.normal(ks[1], (BATCH, MEM_LEN, D_MODEL), f32)
    positions = jnp.broadcast_to(jnp.arange(SEQ, dtype=jnp.int32)[None, :], (BATCH, SEQ)).astype(jnp.int32)
    return {
        "x": x,
        "mem": mem,
        "positions": positions,
        "w_in_ret": w(ks[2], (N_RET, D_MODEL, RET_IN), D_MODEL),
        "w_in_fox": w(ks[3], (N_FOX, D_MODEL, FOX_IN), D_MODEL),
        "b_forget": 0.02 * jax.random.normal(ks[4], (N_FOX, FOX_HEADS), f32),
        "w_mem_kv": w(ks[5], (DEPTH, D_MODEL, 2 * MEM_WIDTH), D_MODEL),
        "w_out": w(ks[6], (DEPTH, D_MODEL, D_MODEL), D_MODEL),
        "w_up": w(ks[7], (DEPTH, D_MODEL, D_FF), D_MODEL),
        "w_down": w(ks[8], (DEPTH, D_FF, D_MODEL), D_FF),
        "g_pre_mix": gain(ks[9], (DEPTH, D_MODEL)),
        "g_post_mix": gain(ks[10], (DEPTH, D_MODEL)),
        "g_pre_mlp": gain(ks[11], (DEPTH, D_MODEL)),
        "g_post_mlp": gain(ks[12], (DEPTH, D_MODEL)),
        "g_mem": gain(ks[13], (DEPTH, D_MODEL)),
    }


def reference(x, mem, positions, w_in_ret, w_in_fox, b_forget, w_mem_kv, w_out, w_up, w_down,
              g_pre_mix, g_post_mix, g_pre_mlp, g_post_mlp, g_mem):
    B, S, _ = x.shape
    for i in range(DEPTH):
        mixer = i % N_MIXERS
        j = i // N_MIXERS
        h = rmsnorm(x, g_pre_mix[i])
        if mixer == 0:
            proj = h @ w_in_ret[j]
            q, k, v, gate, mq = jnp.split(proj, [RET_WIDTH, 2 * RET_WIDTH, 3 * RET_WIDTH, 4 * RET_WIDTH], axis=-1)
            q = rotary(q.reshape(B, S, RET_HEADS, RET_HEAD_DIM), positions)
            k = rotary(k.reshape(B, S, RET_HEADS, RET_HEAD_DIM), positions)
            v = v.reshape(B, S, RET_HEADS, RET_HEAD_DIM).astype(jnp.float32)
            r = retention_chunkwise(q, k, v)
            r = r * lax.rsqrt(jnp.mean(r * r, axis=-1, keepdims=True) + EPS)
            self_out = (r.reshape(B, S, RET_WIDTH) * jax.nn.silu(gate.astype(jnp.float32))).astype(x.dtype)
        else:
            proj = h @ w_in_fox[j]
            q, k, v, f_logit, mq = jnp.split(proj, [FOX_WIDTH, 2 * FOX_WIDTH, 3 * FOX_WIDTH, 3 * FOX_WIDTH + FOX_HEADS], axis=-1)
            log_f = jax.nn.log_sigmoid(f_logit.astype(jnp.float32) + b_forget[j].astype(jnp.float32))
            self_out = forgetting_attention(q.reshape(B, S, FOX_HEADS, FOX_HEAD_DIM),
                                            k.reshape(B, S, FOX_HEADS, FOX_HEAD_DIM),
                                            v.reshape(B, S, FOX_HEADS, FOX_HEAD_DIM), log_f)
            self_out = self_out.reshape(B, S, FOX_WIDTH).astype(x.dtype)
        mem_out = memory_attention(mq.reshape(B, S, MEM_HEADS, MEM_HEAD_DIM), mem, g_mem[i], w_mem_kv[i])
        mixed = jnp.concatenate([self_out, mem_out.reshape(B, S, MEM_WIDTH).astype(x.dtype)], axis=-1)
        x = x + rmsnorm(mixed @ w_out[i], g_post_mix[i])
        h = rmsnorm(x, g_pre_mlp[i])
        u = jax.nn.relu(h @ w_up[i])
        x = x + rmsnorm((u * u) @ w_down[i], g_post_mlp[i])
    return x
```

```python
import functools
import math

import jax
import jax.numpy as jnp
from jax import lax
from jax.experimental import pallas as pl
from jax.experimental.pallas import tpu as pltpu

D_MODEL = 1024
MEM_HEADS = 4
MEM_HEAD_DIM = 64
MEM_WIDTH = MEM_HEADS * MEM_HEAD_DIM
RET_HEAD_DIM = 128
RET_HEADS = 6
RET_WIDTH = RET_HEADS * RET_HEAD_DIM
RET_CHUNK = 128
ROPE_BASE = 10000.0
FOX_HEAD_DIM = 64
FOX_HEADS = 12
FOX_WIDTH = FOX_HEADS * FOX_HEAD_DIM
D_FF = 4 * D_MODEL
EPS = 1e-6

LANES = 128
MXU_COLS = 256
VMEM_LIMIT_BYTES = 56 * 1024 * 1024
LOG2E = 1.4426950408889634
NEG_BIG = -1e30

ROW_TILE = 512
FF_CHUNK = 512
FOX_TILE = 256
CUM_TILE = 256
MEM_ROWS = 256

N_SELF_BLOCKS = RET_WIDTH // LANES
N_MEM_BLOCKS = MEM_WIDTH // LANES

_NT = (((1,), (1,)), ((), ()))
_F32 = jnp.float32
_BF16 = jnp.bfloat16


def _rms_scale(x):
    return lax.rsqrt(jnp.mean(x * x, axis=-1, keepdims=True) + EPS)


def _const_spec(shape):
    nd = len(shape)
    return pl.BlockSpec(shape, lambda *_: (0,) * nd, pipeline_mode=pl.Buffered(1))


def _rope_kernel(pos_ref, inv_ref, cos_ref, sin_ref):
    ang = pos_ref[...] * inv_ref[...]
    cos_ref[...] = jnp.cos(ang)
    sin_ref[...] = jnp.sin(ang)


def _rope_tables(positions):
    t = positions.size
    half = RET_HEAD_DIM // 2
    inv_freq = ROPE_BASE ** (-jnp.arange(0, RET_HEAD_DIM, 2, dtype=_F32) / RET_HEAD_DIM)
    pos2 = jnp.repeat(positions.reshape(t // 2, 2).astype(_F32), half, axis=1)
    inv2 = jnp.tile(inv_freq, 2).reshape(1, LANES)
    rows = math.gcd(t // 2, 1024)
    cos2, sin2 = pl.pallas_call(
        _rope_kernel,
        out_shape=[jax.ShapeDtypeStruct((t // 2, LANES), _F32)] * 2,
        grid=(t // 2 // rows,),
        in_specs=[pl.BlockSpec((rows, LANES), lambda i: (i, 0)),
                  pl.BlockSpec((1, LANES), lambda i: (0, 0))],
        out_specs=[pl.BlockSpec((rows, LANES), lambda i: (i, 0))] * 2,
        name="rope_tables",
    )(pos2, inv2)
    cos = cos2.reshape(t, half)
    sin = sin2.reshape(t, half)
    return jnp.concatenate([cos, cos], axis=1), jnp.concatenate([-sin, sin], axis=1)


def _inproj_kernel(*refs, epilogues, has_rope, n_flog):
    if has_rope:
        x_ref, g_ref, w_ref, cos_ref, sin_ref = refs[:5]
        rest = refs[5:]
    else:
        x_ref, g_ref, w_ref = refs[:3]
        rest = refs[3:]
    o_ref = rest[0]
    x = x_ref[...]
    h = (x * _rms_scale(x) * g_ref[...]).astype(_BF16)
    nb = len(epilogues)

    def finish(j, a):
        kind, scale = epilogues[j]
        if kind == "rot":
            a = a * cos_ref[...] + pltpu.roll(a, RET_HEAD_DIM // 2, 1) * sin_ref[...]
        if scale != 1.0:
            a = a * scale
        o_ref[j] = a.astype(o_ref.dtype)

    for jb in range(nb // 2):
        acc = jnp.dot(h, w_ref[:, jb * MXU_COLS:(jb + 1) * MXU_COLS], preferred_element_type=_F32)
        finish(2 * jb, acc[:, :LANES])
        finish(2 * jb + 1, acc[:, LANES:])
    if n_flog:
        flog_ref = rest[1]
        base = nb * LANES
        flog_ref[...] = jnp.dot(h, w_ref[:, base:base + LANES], preferred_element_type=_F32)


def _inproj(x2, g, w_cat, epilogues, rope=None, n_flog=0):
    t = x2.shape[0]
    nb = len(epilogues)
    ncols = w_cat.shape[1]
    in_specs = [pl.BlockSpec((ROW_TILE, D_MODEL), lambda i: (i, 0)),
                _const_spec((1, D_MODEL)),
                _const_spec((D_MODEL, ncols))]
    args = [x2, g.reshape(1, D_MODEL), w_cat]
    if rope is not None:
        in_specs += [pl.BlockSpec((ROW_TILE, LANES), lambda i: (i, 0))] * 2
        args += list(rope)
    out_shape = [jax.ShapeDtypeStruct((nb, t, LANES), _BF16)]
    out_specs = [pl.BlockSpec((nb, ROW_TILE, LANES), lambda i: (0, i, 0))]
    if n_flog:
        out_shape.append(jax.ShapeDtypeStruct((t, LANES), _F32))
        out_specs.append(pl.BlockSpec((ROW_TILE, LANES), lambda i: (i, 0)))
    return pl.pallas_call(
        functools.partial(_inproj_kernel, epilogues=tuple(epilogues), has_rope=rope is not None, n_flog=n_flog),
        out_shape=out_shape,
        grid=(t // ROW_TILE,),
        in_specs=in_specs,
        out_specs=out_specs,
        compiler_params=pltpu.CompilerParams(dimension_semantics=("arbitrary",), vmem_limit_bytes=VMEM_LIMIT_BYTES),
        name="inproj_rope" if rope is not None else "inproj",
    )(*args)


def _retention_kernel(q_ref, k_ref, v_ref, gate_ref, dmask_ref, qd_ref, kd_ref, cd_ref, o_ref, *, n_chunks):
    c = RET_CHUNK
    dmask = dmask_ref[0]
    qd = qd_ref[0]
    kd = kd_ref[0]
    cd = cd_ref[0]

    def step(n, state):
        rows = pl.ds(pl.multiple_of(n * c, c), c)
        qc = q_ref[0, rows, :]
        kc = k_ref[0, rows, :]
        vc = v_ref[0, rows, :]
        inner = lax.dot_general(qc, kc, _NT, preferred_element_type=_F32) * dmask
        q_scaled = (qc.astype(_F32) * qd).astype(_BF16)
        lhs = jnp.concatenate([inner.astype(_BF16), q_scaled], axis=1)
        rhs = jnp.concatenate([vc, state.astype(_BF16)], axis=0)
        out = jnp.dot(lhs, rhs, preferred_element_type=_F32)
        k_scaled_t = (kc.astype(_F32) * kd).T.astype(_BF16)
        state = state * cd + jnp.dot(k_scaled_t, vc, preferred_element_type=_F32)
        r = out * _rms_scale(out)
        g = gate_ref[0, rows, :].astype(_F32)
        o_ref[0, rows, :] = (r * (g * jax.nn.sigmoid(g))).astype(o_ref.dtype)
        return state

    lax.fori_loop(0, n_chunks, step, jnp.zeros((RET_HEAD_DIM, RET_HEAD_DIM), _F32))


def _retention_tables():
    c = RET_CHUNK
    log_gamma = jnp.log(1.0 - 2.0 ** (-5.0 - jnp.arange(RET_HEADS, dtype=_F32)))
    idx = jnp.arange(c, dtype=_F32)
    rel = idx[:, None] - idx[None, :]
    dmask = jnp.where(rel[None] >= 0, jnp.exp(jnp.maximum(rel, 0.0)[None] * log_gamma[:, None, None]), 0.0)
    q_decay = jnp.exp((idx[None, :] + 1.0) * log_gamma[:, None])
    k_decay = jnp.exp((c - 1.0 - idx[None, :]) * log_gamma[:, None])
    chunk_decay = jnp.exp(c * log_gamma)
    full = (RET_HEADS, c, RET_HEAD_DIM)
    return (dmask,
            jnp.broadcast_to(q_decay[:, :, None], full),
            jnp.broadcast_to(k_decay[:, :, None], full),
            jnp.broadcast_to(chunk_decay[:, None, None], full))


def _retention(proj, batch, seq):
    nh = RET_HEADS
    p4 = proj.reshape(proj.shape[0], batch, seq, LANES)

    def slab(offset):
        return pl.BlockSpec((1, 1, seq, LANES), lambda b, h: (offset + h, b, 0, 0))

    def table():
        return pl.BlockSpec((1, RET_CHUNK, RET_HEAD_DIM), lambda b, h: (h, 0, 0))

    def body(q_ref, k_ref, v_ref, gate_ref, dm, qd, kd, cd, o_ref):
        _retention_kernel(q_ref.at[0], k_ref.at[0], v_ref.at[0], gate_ref.at[0], dm, qd, kd, cd, o_ref.at[0],
                          n_chunks=seq // RET_CHUNK)

    out = pl.pallas_call(
        body,
        out_shape=jax.ShapeDtypeStruct((nh, batch, seq, LANES), _BF16),
        grid=(batch, nh),
        in_specs=[slab(0), slab(nh), slab(2 * nh), slab(3 * nh), table(), table(), table(), table()],
        out_specs=pl.BlockSpec((1, 1, seq, LANES), lambda b, h: (h, b, 0, 0)),
        compiler_params=pltpu.CompilerParams(dimension_semantics=("arbitrary", "arbitrary"),
                                             vmem_limit_bytes=VMEM_LIMIT_BYTES),
        name="retention",
    )(p4, p4, p4, p4, *_retention_tables())
    return out.reshape(nh, batch * seq, LANES)


def _split3(x):
    hi = x.astype(_BF16)
    r1 = x - hi.astype(_F32)
    mid = r1.astype(_BF16)
    lo = (r1 - mid.astype(_F32)).astype(_BF16)
    return hi, mid, lo


def _fox_kernel(q_ref, k_ref, v_ref, flog_ref, bias_ref, eq_ref, ek_ref, tri_ref, o_ref,
                cparts_ref, qa_ref, ka_ref, va_ref, *, seq):
    pair = pl.program_id(1)
    lane = lax.broadcasted_iota(jnp.int32, (1, LANES), 1)
    low_half = lane < FOX_HEAD_DIM

    @pl.when(pair == 0)
    def _():
        tri = tri_ref[...]

        def cum_step(i, carry):
            rows = pl.ds(pl.multiple_of(i * CUM_TILE, CUM_TILE), CUM_TILE)
            z = flog_ref[0, rows, :] + bias_ref[...]
            log_f = jnp.minimum(z, 0.0) - jnp.log1p(jnp.exp(-jnp.abs(z)))
            hi, mid, lo = _split3(log_f)
            cs = (jnp.dot(tri, hi, preferred_element_type=_F32)
                  + jnp.dot(tri, mid, preferred_element_type=_F32)
                  + jnp.dot(tri, lo, preferred_element_type=_F32)) + carry
            c_hi, c_mid, c_lo = _split3(cs * LOG2E)
            c_lo = jnp.where(lane == LANES - 1, jnp.ones_like(c_lo), c_lo)
            cparts_ref[rows, 0:LANES] = c_hi
            cparts_ref[rows, LANES:2 * LANES] = c_mid
            cparts_ref[rows, 2 * LANES:3 * LANES] = c_lo
            return cs[CUM_TILE - 1:CUM_TILE, :]

        lax.fori_loop(0, seq // CUM_TILE, cum_step, jnp.zeros((1, LANES), _F32))

    cparts = cparts_ref[...]
    for e in range(2):
        keep = low_half if e == 0 else jnp.logical_not(low_half)
        extra_q = jnp.dot(cparts, eq_ref[e], preferred_element_type=_F32).astype(_BF16)
        extra_k = jnp.dot(cparts, ek_ref[e], preferred_element_type=_F32).astype(_BF16)
        qa_ref[e] = jnp.where(keep, q_ref[0], extra_q)
        ka_ref[e] = jnp.where(keep, k_ref[0], extra_k)
        va_ref[e] = jnp.where(keep, v_ref[0], jnp.zeros_like(v_ref[0]))

    t = FOX_TILE
    row_id = lax.broadcasted_iota(jnp.int32, (t, t), 0)
    col_id = lax.broadcasted_iota(jnp.int32, (t, t), 1)
    causal = col_id <= row_id

    def q_block(qi, _):
        q_rows = pl.ds(pl.multiple_of(qi * t, t), t)
        qa = [qa_ref[e, q_rows, :] for e in range(2)]

        def update(kj, carry, masked):
            k_rows = pl.ds(pl.multiple_of(kj * t, t), t)
            new = []
            probs = []
            for e in range(2):
                m_prev, l_prev = carry[2 * e], carry[2 * e + 1]
                s = lax.dot_general(qa[e], ka_ref[e, k_rows, :], _NT, preferred_element_type=_F32)
                if masked:
                    s = jnp.where(causal, s, NEG_BIG)
                m_new = jnp.maximum(m_prev, jnp.max(s, axis=-1, keepdims=True))
                p = jnp.exp2(s - m_new)
                alpha = jnp.exp2(m_prev - m_new)
                new += [m_new, alpha * l_prev + jnp.sum(p, axis=-1, keepdims=True)]
                probs.append((p.astype(_BF16), alpha))
            acc = carry[4]
            alpha_lanes = jnp.where(low_half, probs[0][1], probs[1][1])
            p_cat = jnp.concatenate([probs[0][0], probs[1][0]], axis=1)
            v_cat = jnp.concatenate([va_ref[0, k_rows, :], va_ref[1, k_rows, :]], axis=0)
            acc = acc * alpha_lanes + jnp.dot(p_cat, v_cat, preferred_element_type=_F32)
            return tuple(new) + (acc,)

        init = (jnp.full((t, 1), NEG_BIG, _F32), jnp.zeros((t, 1), _F32),
                jnp.full((t, 1), NEG_BIG, _F32), jnp.zeros((t, 1), _F32),
                jnp.zeros((t, LANES), _F32))
        carry = lax.fori_loop(0, qi, lambda kj, cr: update(kj, cr, False), init)
        carry = update(qi, carry, True)
        inv_l = jnp.where(low_half, 1.0 / carry[1], 1.0 / carry[3])
        o_ref[0, q_rows, :] = (carry[4] * inv_l).astype(o_ref.dtype)
        return 0

    lax.fori_loop(0, seq // t, q_block, 0)


def _fox_selectors():
    import numpy as np
    eq = np.zeros((FOX_HEADS, 3 * LANES, LANES), np.float32)
    ek = np.zeros((FOX_HEADS, 3 * LANES, LANES), np.float32)
    ones_row = 3 * LANES - 1
    for h in range(FOX_HEADS):
        base = FOX_HEAD_DIM if h % 2 == 0 else 0
        for part in range(3):
            eq[h, part * LANES + h, base + part] = 1.0
            ek[h, ones_row, base + part] = 1.0
            eq[h, ones_row, base + 3 + part] = 1.0
            ek[h, part * LANES + h, base + 3 + part] = -1.0
    return jnp.asarray(eq, _BF16), jnp.asarray(ek, _BF16)


def _fox(proj, flog, b_forget, batch, seq):
    npair = FOX_HEADS // 2
    p4 = proj.reshape(proj.shape[0], batch, seq, LANES)
    flog3 = flog.reshape(batch, seq, LANES)
    bias = jnp.zeros((1, LANES), _F32).at[0, :FOX_HEADS].set(b_forget.astype(_F32))
    eq, ek = _fox_selectors()
    tri = jnp.tril(jnp.ones((CUM_TILE, CUM_TILE), _F32)).astype(_BF16)

    def slab(offset):
        return pl.BlockSpec((1, 1, seq, LANES), lambda b, p: (offset + p, b, 0, 0))

    def body(q_ref, k_ref, v_ref, flog_ref, bias_ref, eq_ref, ek_ref, tri_ref, o_ref, *scratch):
        _fox_kernel(q_ref.at[0], k_ref.at[0], v_ref.at[0], flog_ref, bias_ref, eq_ref, ek_ref, tri_ref,
                    o_ref.at[0], *scratch, seq=seq)

    out = pl.pallas_call(
        body,
        out_shape=jax.ShapeDtypeStruct((npair, batch, seq, LANES), _BF16),
        grid=(batch, npair),
        in_specs=[slab(0), slab(npair), slab(2 * npair),
                  pl.BlockSpec((1, seq, LANES), lambda b, p: (b, 0, 0)),
                  _const_spec((1, LANES)),
                  pl.BlockSpec((2, 3 * LANES, LANES), lambda b, p: (p, 0, 0)),
                  pl.BlockSpec((2, 3 * LANES, LANES), lambda b, p: (p, 0, 0)),
                  _const_spec((CUM_TILE, CUM_TILE))],
        out_specs=pl.BlockSpec((1, 1, seq, LANES), lambda b, p: (p, b, 0, 0)),
        scratch_shapes=[pltpu.VMEM((seq, 3 * LANES), _BF16),
                        pltpu.VMEM((2, seq, LANES), _BF16),
                        pltpu.VMEM((2, seq, LANES), _BF16),
                        pltpu.VMEM((2, seq, LANES), _BF16)],
        compiler_params=pltpu.CompilerParams(dimension_semantics=("arbitrary", "arbitrary"),
                                             vmem_limit_bytes=VMEM_LIMIT_BYTES),
        name="fox_attention",
    )(p4, p4, p4, flog3, bias, eq, ek, tri)
    return out.reshape(npair, batch * seq, LANES)


def _memattn_kernel(mem_ref, g_ref, w_ref, mq_ref, o_ref, *, seq):
    mem = mem_ref[0]
    kvn = (mem * _rms_scale(mem) * g_ref[...]).astype(_BF16)
    kv = jnp.dot(kvn, w_ref[...], preferred_element_type=_F32)
    lane = lax.broadcasted_iota(jnp.int32, (1, LANES), 1)
    low_half = lane < MEM_HEAD_DIM
    scale = MEM_HEAD_DIM ** -0.5
    for p in range(N_MEM_BLOCKS):
        mk = (kv[:, p * LANES:(p + 1) * LANES] * scale).astype(_BF16)
        mv = kv[:, MEM_WIDTH + p * LANES:MEM_WIDTH + (p + 1) * LANES].astype(_BF16)
        zero = jnp.zeros_like(mk)
        mk_a, mk_b = jnp.where(low_half, mk, zero), jnp.where(low_half, zero, mk)
        mv_cat = jnp.concatenate([jnp.where(low_half, mv, zero), jnp.where(low_half, zero, mv)], axis=0)

        def rows_step(i, _, p=p, mk_a=mk_a, mk_b=mk_b, mv_cat=mv_cat):
            rows = pl.ds(pl.multiple_of(i * MEM_ROWS, MEM_ROWS), MEM_ROWS)
            mq = mq_ref[p, rows, :]
            probs, inv = [], []
            for mk_e in (mk_a, mk_b):
                s = lax.dot_general(mq, mk_e, _NT, preferred_element_type=_F32)
                e = jnp.exp(s - jnp.max(s, axis=-1, keepdims=True))
                probs.append(e.astype(_BF16))
                inv.append(1.0 / jnp.sum(e, axis=-1, keepdims=True))
            out = jnp.dot(jnp.concatenate(probs, axis=1), mv_cat, preferred_element_type=_F32)
            o_ref[p, rows, :] = (out * jnp.where(low_half, inv[0], inv[1])).astype(o_ref.dtype)
            return 0

        lax.fori_loop(0, seq // MEM_ROWS, rows_step, 0)


def _memattn(proj, mq_offset, mem, g_mem, w_kv, batch, seq):
    p4 = proj.reshape(proj.shape[0], batch, seq, LANES)
    mem_len = mem.shape[1]

    def body(mem_ref, g_ref, w_ref, mq_ref, o_ref):
        _memattn_kernel(mem_ref, g_ref, w_ref, mq_ref.at[:, 0], o_ref.at[:, 0], seq=seq)

    out = pl.pallas_call(
        body,
        out_shape=jax.ShapeDtypeStruct((N_MEM_BLOCKS, batch, seq, LANES), _BF16),
        grid=(batch,),
        in_specs=[pl.BlockSpec((1, mem_len, D_MODEL), lambda b: (b, 0, 0)),
                  _const_spec((1, D_MODEL)),
                  _const_spec((D_MODEL, 2 * MEM_WIDTH)),
                  pl.BlockSpec((N_MEM_BLOCKS, 1, seq, LANES), lambda b: (mq_offset // N_MEM_BLOCKS, b, 0, 0))],
        out_specs=pl.BlockSpec((N_MEM_BLOCKS, 1, seq, LANES), lambda b: (0, b, 0, 0)),
        compiler_params=pltpu.CompilerParams(dimension_semantics=("arbitrary",), vmem_limit_bytes=VMEM_LIMIT_BYTES),
        name="memory_attention",
    )(mem, g_mem.reshape(1, D_MODEL), w_kv, p4)
    return out.reshape(N_MEM_BLOCKS, batch * seq, LANES)


def _tail_kernel(x_ref, so_ref, mo_ref, wo_ref, g1_ref, g2_ref, g3_ref, wup_ref, wdn_ref, o_ref):
    mixed = jnp.concatenate([so_ref[j] for j in range(N_SELF_BLOCKS)] + [mo_ref[j] for j in range(N_MEM_BLOCKS)],
                            axis=1)
    y = jnp.dot(mixed, wo_ref[...], preferred_element_type=_F32)
    x1 = x_ref[...] + y * _rms_scale(y) * g1_ref[...]
    h = (x1 * _rms_scale(x1) * g2_ref[...]).astype(_BF16)
    acc = jnp.zeros_like(x1)
    for c in range(D_FF // FF_CHUNK):
        cols = slice(c * FF_CHUNK, (c + 1) * FF_CHUNK)
        u = jnp.maximum(jnp.dot(h, wup_ref[:, cols], preferred_element_type=_F32), 0.0)
        acc = acc + jnp.dot((u * u).astype(_BF16), wdn_ref[cols, :], preferred_element_type=_F32)
    o_ref[...] = x1 + acc * _rms_scale(acc) * g3_ref[...]


def _tail(x2, self_out, mem_out, w_out, g_post_mix, g_pre_mlp, g_post_mlp, w_up, w_down):
    t = x2.shape[0]
    row = lambda: pl.BlockSpec((ROW_TILE, D_MODEL), lambda i: (i, 0))
    gain = lambda g: g.reshape(1, D_MODEL)
    return pl.pallas_call(
        _tail_kernel,
        out_shape=jax.ShapeDtypeStruct((t, D_MODEL), _F32),
        grid=(t // ROW_TILE,),
        in_specs=[row(),
                  pl.BlockSpec((N_SELF_BLOCKS, ROW_TILE, LANES), lambda i: (0, i, 0)),
                  pl.BlockSpec((N_MEM_BLOCKS, ROW_TILE, LANES), lambda i: (0, i, 0)),
                  _const_spec((D_MODEL, D_MODEL)),
                  _const_spec((1, D_MODEL)), _const_spec((1, D_MODEL)), _const_spec((1, D_MODEL)),
                  _const_spec((D_MODEL, D_FF)), _const_spec((D_FF, D_MODEL))],
        out_specs=row(),
        compiler_params=pltpu.CompilerParams(dimension_semantics=("arbitrary",), vmem_limit_bytes=VMEM_LIMIT_BYTES),
        name="outproj_mlp",
    )(x2, self_out, mem_out, w_out, gain(g_post_mix), gain(g_pre_mlp), gain(g_post_mlp), w_up, w_down)


def kernel(x, mem, positions, w_in_ret, w_in_fox, b_forget, w_mem_kv, w_out, w_up, w_down,
           g_pre_mix, g_post_mix, g_pre_mlp, g_post_mlp, g_mem):
    batch, seq, _ = x.shape
    depth = w_out.shape[0]
    assert seq % ROW_TILE == 0 and seq % FOX_TILE == 0 and seq % RET_CHUNK == 0
    x2 = x.reshape(batch * seq, D_MODEL)
    rope = _rope_tables(positions)
    bf = lambda w: w.astype(_BF16)
    for i in range(depth):
        j = i // 2
        if i % 2 == 0:
            epi = ([("rot", 1.0)] * RET_HEADS + [("rot", RET_HEAD_DIM ** -0.5)] * RET_HEADS
                   + [("plain", 1.0)] * (2 * RET_HEADS + N_MEM_BLOCKS))
            (proj,) = _inproj(x2, g_pre_mix[i], bf(w_in_ret[j]), epi, rope=rope)
            self_out = _retention(proj, batch, seq)
            mq_offset = 4 * RET_HEADS
        else:
            w = w_in_fox[j]
            w_f = jnp.zeros((D_MODEL, LANES), w.dtype).at[:, :FOX_HEADS].set(w[:, 3 * FOX_WIDTH:3 * FOX_WIDTH + FOX_HEADS])
            w_cat = jnp.concatenate([w[:, :3 * FOX_WIDTH], w[:, 3 * FOX_WIDTH + FOX_HEADS:], w_f], axis=1)
            npair = FOX_HEADS // 2
            epi = ([("plain", FOX_HEAD_DIM ** -0.5 * LOG2E)] * npair + [("plain", 1.0)] * (2 * npair + N_MEM_BLOCKS))
            proj, flog = _inproj(x2, g_pre_mix[i], bf(w_cat), epi, n_flog=1)
            self_out = _fox(proj, flog, b_forget[j], batch, seq)
            mq_offset = 3 * npair
        mem_out = _memattn(proj, mq_offset, mem, g_mem[i], bf(w_mem_kv[i]), batch, seq)
        x2 = _tail(x2, self_out, mem_out, bf(w_out[i]), g_post_mix[i], g_pre_mlp[i], g_post_mlp[i],
                   bf(w_up[i]), bf(w_down[i]))
    return x2.reshape(batch, seq, D_MODEL)
```

```python
import functools
import math

import jax
import jax.numpy as jnp
from jax import lax
from jax.experimental import pallas as pl
from jax.experimental.pallas import tpu as pltpu

D_MODEL = 1024
MEM_HEADS = 4
MEM_HEAD_DIM = 64
MEM_WIDTH = MEM_HEADS * MEM_HEAD_DIM
RET_HEAD_DIM = 128
RET_HEADS = 6
RET_WIDTH = RET_HEADS * RET_HEAD_DIM
RET_CHUNK = 128
ROPE_BASE = 10000.0
FOX_HEAD_DIM = 64
FOX_HEADS = 12
FOX_WIDTH = FOX_HEADS * FOX_HEAD_DIM
D_FF = 4 * D_MODEL
EPS = 1e-6

LANES = 128
MXU_COLS = 256
VMEM_LIMIT_BYTES = 56 * 1024 * 1024
LOG2E = 1.4426950408889634
NEG_BIG = -1e30

ROW_TILE = 512
FF_CHUNK = 512
FOX_TILE = 256
CUM_TILE = 256
MEM_ROWS = 256

N_SELF_BLOCKS = RET_WIDTH // LANES
N_MEM_BLOCKS = MEM_WIDTH // LANES

_NT = (((1,), (1,)), ((), ()))
_F32 = jnp.float32
_BF16 = jnp.bfloat16


def _rms_scale(x):
    return lax.rsqrt(jnp.mean(x * x, axis=-1, keepdims=True) + EPS)


def _const_spec(shape):
    nd = len(shape)
    return pl.BlockSpec(shape, lambda *_: (0,) * nd, pipeline_mode=pl.Buffered(1))


def _rope_kernel(pos_ref, inv_ref, cos_ref, sin_ref):
    ang = pos_ref[...] * inv_ref[...]
    cos_ref[...] = jnp.cos(ang)
    sin_ref[...] = jnp.sin(ang)


def _rope_tables(positions):
    t = positions.size
    half = RET_HEAD_DIM // 2
    inv_freq = ROPE_BASE ** (-jnp.arange(0, RET_HEAD_DIM, 2, dtype=_F32) / RET_HEAD_DIM)
    pos2 = jnp.repeat(positions.reshape(t // 2, 2).astype(_F32), half, axis=1)
    inv2 = jnp.tile(inv_freq, 2).reshape(1, LANES)
    rows = math.gcd(t // 2, 1024)
    cos2, sin2 = pl.pallas_call(
        _rope_kernel,
        out_shape=[jax.ShapeDtypeStruct((t // 2, LANES), _F32)] * 2,
        grid=(t // 2 // rows,),
        in_specs=[pl.BlockSpec((rows, LANES), lambda i: (i, 0)),
                  pl.BlockSpec((1, LANES), lambda i: (0, 0))],
        out_specs=[pl.BlockSpec((rows, LANES), lambda i: (i, 0))] * 2,
        name="rope_tables",
    )(pos2, inv2)
    cos = cos2.reshape(t, half)
    sin = sin2.reshape(t, half)
    return jnp.concatenate([cos, cos], axis=1), jnp.concatenate([-sin, sin], axis=1)


def _inproj_kernel(*refs, epilogues, has_rope, n_flog):
    if has_rope:
        x_ref, g_ref, w_ref, cos_ref, sin_ref = refs[:5]
        rest = refs[5:]
    else:
        x_ref, g_ref, w_ref = refs[:3]
        rest = refs[3:]
    o_ref = rest[0]
    x = x_ref[...]
    h = (x * _rms_scale(x) * g_ref[...]).astype(_BF16)
    nb = len(epilogues)

    def finish(j, a):
        kind, scale = epilogues[j]
        if kind == "rot":
            a = a * cos_ref[...] + pltpu.roll(a, RET_HEAD_DIM // 2, 1) * sin_ref[...]
        if scale != 1.0:
            a = a * scale
        o_ref[j] = a.astype(o_ref.dtype)

    for jb in range(nb // 2):
        acc = jnp.dot(h, w_ref[:, jb * MXU_COLS:(jb + 1) * MXU_COLS], preferred_element_type=_F32)
        finish(2 * jb, acc[:, :LANES])
        finish(2 * jb + 1, acc[:, LANES:])
    if n_flog:
        flog_ref = rest[1]
        base = nb * LANES
        flog_ref[...] = jnp.dot(h, w_ref[:, base:base + LANES], preferred_element_type=_F32)


def _inproj(x2, g, w_cat, epilogues, rope=None, n_flog=0):
    t = x2.shape[0]
    nb = len(epilogues)
    ncols = w_cat.shape[1]
    in_specs = [pl.BlockSpec((ROW_TILE, D_MODEL), lambda i: (i, 0)),
                _const_spec((1, D_MODEL)),
                _const_spec((D_MODEL, ncols))]
    args = [x2, g.reshape(1, D_MODEL), w_cat]
    if rope is not None:
        in_specs += [pl.BlockSpec((ROW_TILE, LANES), lambda i: (i, 0))] * 2
        args += list(rope)
    out_shape = [jax.ShapeDtypeStruct((nb, t, LANES), _BF16)]
    out_specs = [pl.BlockSpec((nb, ROW_TILE, LANES), lambda i: (0, i, 0))]
    if n_flog:
        out_shape.append(jax.ShapeDtypeStruct((t, LANES), _F32))
        out_specs.append(pl.BlockSpec((ROW_TILE, LANES), lambda i: (i, 0)))
    return pl.pallas_call(
        functools.partial(_inproj_kernel, epilogues=tuple(epilogues), has_rope=rope is not None, n_flog=n_flog),
        out_shape=out_shape,
        grid=(t // ROW_TILE,),
        in_specs=in_specs,
        out_specs=out_specs,
        compiler_params=pltpu.CompilerParams(dimension_semantics=("arbitrary",), vmem_limit_bytes=VMEM_LIMIT_BYTES),
        name="inproj_rope" if rope is not None else "inproj",
    )(*args)


def _retention_kernel(q_ref, k_ref, v_ref, gate_ref, dmask_ref, qd_ref, kd_ref, cd_ref, o_ref, *, n_chunks):
    c = RET_CHUNK
    dmask = dmask_ref[0]
    qd = qd_ref[0]
    kd = kd_ref[0]
    cd = cd_ref[0]

    def step(n, state):
        rows = pl.ds(n * c, c)
        qc = q_ref[0, rows, :]
        kc = k_ref[0, rows, :]
        vc = v_ref[0, rows, :]
        inner = lax.dot_general(qc, kc, _NT, preferred_element_type=_F32) * dmask
        q_scaled = (qc.astype(_F32) * qd).astype(_BF16)
        lhs = jnp.concatenate([inner.astype(_BF16), q_scaled], axis=1)
        rhs = jnp.concatenate([vc, state.astype(_BF16)], axis=0)
        out = jnp.dot(lhs, rhs, preferred_element_type=_F32)
        k_scaled_t = (kc.astype(_F32) * kd).T.astype(_BF16)
        state = state * cd + jnp.dot(k_scaled_t, vc, preferred_element_type=_F32)
        r = out * _rms_scale(out)
        g = gate_ref[0, rows, :].astype(_F32)
        o_ref[0, rows, :] = (r * (g * jax.nn.sigmoid(g))).astype(o_ref.dtype)
        return state

    state = jnp.zeros((RET_HEAD_DIM, RET_HEAD_DIM), _F32)
    for n in range(n_chunks):
        state = step(n, state)


def _retention_tables():
    c = RET_CHUNK
    log_gamma = jnp.log(1.0 - 2.0 ** (-5.0 - jnp.arange(RET_HEADS, dtype=_F32)))
    idx = jnp.arange(c, dtype=_F32)
    rel = idx[:, None] - idx[None, :]
    dmask = jnp.where(rel[None] >= 0, jnp.exp(jnp.maximum(rel, 0.0)[None] * log_gamma[:, None, None]), 0.0)
    q_decay = jnp.exp((idx[None, :] + 1.0) * log_gamma[:, None])
    k_decay = jnp.exp((c - 1.0 - idx[None, :]) * log_gamma[:, None])
    chunk_decay = jnp.exp(c * log_gamma)
    full = (RET_HEADS, c, RET_HEAD_DIM)
    return (dmask,
            jnp.broadcast_to(q_decay[:, :, None], full),
            jnp.broadcast_to(k_decay[:, :, None], full),
            jnp.broadcast_to(chunk_decay[:, None, None], full))


def _retention(proj, batch, seq):
    nh = RET_HEADS
    p4 = proj.reshape(proj.shape[0], batch, seq, LANES)

    def slab(offset):
        return pl.BlockSpec((1, 1, seq, LANES), lambda b, h: (offset + h, b, 0, 0))

    def table():
        return pl.BlockSpec((1, RET_CHUNK, RET_HEAD_DIM), lambda b, h: (h, 0, 0))

    def body(q_ref, k_ref, v_ref, gate_ref, dm, qd, kd, cd, o_ref):
        _retention_kernel(q_ref.at[0], k_ref.at[0], v_ref.at[0], gate_ref.at[0], dm, qd, kd, cd, o_ref.at[0],
                          n_chunks=seq // RET_CHUNK)

    out = pl.pallas_call(
        body,
        out_shape=jax.ShapeDtypeStruct((nh, batch, seq, LANES), _BF16),
        grid=(batch, nh),
        in_specs=[slab(0), slab(nh), slab(2 * nh), slab(3 * nh), table(), table(), table(), table()],
        out_specs=pl.BlockSpec((1, 1, seq, LANES), lambda b, h: (h, b, 0, 0)),
        compiler_params=pltpu.CompilerParams(dimension_semantics=("arbitrary", "arbitrary"),
                                             vmem_limit_bytes=VMEM_LIMIT_BYTES),
        name="retention",
    )(p4, p4, p4, p4, *_retention_tables())
    return out.reshape(nh, batch * seq, LANES)


def _split3(x):
    hi = x.astype(_BF16)
    r1 = x - hi.astype(_F32)
    mid = r1.astype(_BF16)
    lo = (r1 - mid.astype(_F32)).astype(_BF16)
    return hi, mid, lo


C_PART_STRIDE = 16
ONES_LANE = LANES - 1


def _fox_kernel(q_ref, k_ref, v_ref, flog_ref, bias_ref, sel_ref, tri_ref, o_ref,
                cpack_ref, qa_ref, ka_ref, va_ref, *, seq):
    pair = pl.program_id(1)
    lane = lax.broadcasted_iota(jnp.int32, (1, LANES), 1)
    low_half = lane < FOX_HEAD_DIM
    keep = (low_half, jnp.logical_not(low_half))
    sum_lane = (FOX_HEAD_DIM, 0)

    @pl.when(pair == 0)
    def _():
        tri = tri_ref[...]
        head_lanes = lane < C_PART_STRIDE
        carry = jnp.zeros((1, LANES), _F32)
        for i in range(seq // CUM_TILE):
            rows = pl.ds(i * CUM_TILE, CUM_TILE)
            z = flog_ref[0, rows, :] + bias_ref[...]
            log_f = jnp.minimum(z, 0.0) - jnp.log1p(jnp.exp(-jnp.abs(z)))
            hi, mid, lo = _split3(log_f)
            cs = (jnp.dot(tri, hi, preferred_element_type=_F32)
                  + jnp.dot(tri, mid, preferred_element_type=_F32)
                  + jnp.dot(tri, lo, preferred_element_type=_F32)) + carry
            carry = cs[CUM_TILE - 1:CUM_TILE, :]
            parts = [jnp.where(head_lanes, part.astype(_F32), 0.0) for part in _split3(cs * LOG2E)]
            packed = (parts[0] + pltpu.roll(parts[1], C_PART_STRIDE, 1) + pltpu.roll(parts[2], 2 * C_PART_STRIDE, 1))
            cpack_ref[rows, :] = jnp.where(lane == ONES_LANE, 1.0, packed).astype(_BF16)

    blk = 512
    for r in range(seq // blk):
        rows = pl.ds(r * blk, blk)
        extras = jnp.dot(cpack_ref[rows, :], sel_ref[0], preferred_element_type=_F32).astype(_BF16)
        for e in range(2):
            qa_ref[e, rows, :] = jnp.where(keep[e], q_ref[0, rows, :], extras[:, (2 * e) * LANES:(2 * e + 1) * LANES])
            ka_ref[e, rows, :] = jnp.where(keep[e], k_ref[0, rows, :], extras[:, (2 * e + 1) * LANES:(2 * e + 2) * LANES])
            ones_col = jnp.where(lane == sum_lane[e], 1.0, 0.0).astype(_BF16)
            va_ref[e, rows, :] = jnp.where(keep[e], v_ref[0, rows, :], ones_col)

    t = FOX_TILE
    row_id = lax.broadcasted_iota(jnp.int32, (t, t), 0)
    col_id = lax.broadcasted_iota(jnp.int32, (t, t), 1)
    causal = col_id <= row_id

    for qi in range(seq // t):
        q_rows = pl.ds(qi * t, t)
        acc = []
        for e in range(2):
            qa = qa_ref[e, q_rows, :]
            m_run = jnp.full((t, 1), NEG_BIG, _F32)
            a = jnp.zeros((t, LANES), _F32)
            for kj in range(qi + 1):
                k_rows = pl.ds(kj * t, t)
                s = lax.dot_general(qa, ka_ref[e, k_rows, :], _NT, preferred_element_type=_F32)
                if kj == qi:
                    s = jnp.where(causal, s, NEG_BIG)
                m_new = jnp.maximum(m_run, jnp.max(s, axis=-1, keepdims=True))
                p = jnp.exp2(s - m_new).astype(_BF16)
                a = a * jnp.exp2(m_run - m_new) + jnp.dot(p, va_ref[e, k_rows, :], preferred_element_type=_F32)
                m_run = m_new
            acc.append(a / a[:, sum_lane[e]:sum_lane[e] + 1])
        o_ref[0, q_rows, :] = jnp.where(low_half, acc[0], acc[1]).astype(o_ref.dtype)


def _fox_selectors():
    import numpy as np
    npair = FOX_HEADS // 2
    sel = np.zeros((npair, LANES, 4 * LANES), np.float32)
    for h in range(FOX_HEADS):
        p, e = divmod(h, 2)
        base = FOX_HEAD_DIM if e == 0 else 0
        qcol, kcol = (2 * e) * LANES + base, (2 * e + 1) * LANES + base
        for part in range(3):
            sel[p, part * C_PART_STRIDE + h, qcol + part] = 1.0
            sel[p, ONES_LANE, kcol + part] = 1.0
            sel[p, ONES_LANE, qcol + 3 + part] = 1.0
            sel[p, part * C_PART_STRIDE + h, kcol + 3 + part] = -1.0
    return jnp.asarray(sel, _BF16)


def _fox(proj, flog, b_forget, batch, seq):
    npair = FOX_HEADS // 2
    p4 = proj.reshape(proj.shape[0], batch, seq, LANES)
    flog3 = flog.reshape(batch, seq, LANES)
    bias = jnp.zeros((1, LANES), _F32).at[0, :FOX_HEADS].set(b_forget.astype(_F32))
    tri = jnp.tril(jnp.ones((CUM_TILE, CUM_TILE), _F32)).astype(_BF16)

    def slab(offset):
        return pl.BlockSpec((1, 1, seq, LANES), lambda b, p: (offset + p, b, 0, 0))

    def body(q_ref, k_ref, v_ref, flog_ref, bias_ref, sel_ref, tri_ref, o_ref, *scratch):
        _fox_kernel(q_ref.at[0], k_ref.at[0], v_ref.at[0], flog_ref, bias_ref, sel_ref, tri_ref,
                    o_ref.at[0], *scratch, seq=seq)

    out = pl.pallas_call(
        body,
        out_shape=jax.ShapeDtypeStruct((npair, batch, seq, LANES), _BF16),
        grid=(batch, npair),
        in_specs=[slab(0), slab(npair), slab(2 * npair),
                  pl.BlockSpec((1, seq, LANES), lambda b, p: (b, 0, 0)),
                  _const_spec((1, LANES)),
                  pl.BlockSpec((1, LANES, 4 * LANES), lambda b, p: (p, 0, 0)),
                  _const_spec((CUM_TILE, CUM_TILE))],
        out_specs=pl.BlockSpec((1, 1, seq, LANES), lambda b, p: (p, b, 0, 0)),
        scratch_shapes=[pltpu.VMEM((seq, LANES), _BF16),
                        pltpu.VMEM((2, seq, LANES), _BF16),
                        pltpu.VMEM((2, seq, LANES), _BF16),
                        pltpu.VMEM((2, seq, LANES), _BF16)],
        compiler_params=pltpu.CompilerParams(dimension_semantics=("arbitrary", "arbitrary"),
                                             vmem_limit_bytes=VMEM_LIMIT_BYTES),
        name="fox_attention",
    )(p4, p4, p4, flog3, bias, _fox_selectors(), tri)
    return out.reshape(npair, batch * seq, LANES)


def _memattn_kernel(mem_ref, g_ref, w_ref, mq_ref, o_ref, *, seq):
    mem = mem_ref[0]
    kvn = (mem * _rms_scale(mem) * g_ref[...]).astype(_BF16)
    kv = jnp.dot(kvn, w_ref[...], preferred_element_type=_F32)
    lane = lax.broadcasted_iota(jnp.int32, (1, LANES), 1)
    low_half = lane < MEM_HEAD_DIM
    scale = MEM_HEAD_DIM ** -0.5
    for p in range(N_MEM_BLOCKS):
        mk = (kv[:, p * LANES:(p + 1) * LANES] * scale).astype(_BF16)
        mv = kv[:, MEM_WIDTH + p * LANES:MEM_WIDTH + (p + 1) * LANES].astype(_BF16)
        heads = []
        for e in range(2):
            keep = low_half if e == 0 else jnp.logical_not(low_half)
            sum_lane = MEM_HEAD_DIM if e == 0 else 0
            ones_col = jnp.where(lane == sum_lane, 1.0, 0.0).astype(_BF16)
            heads.append((jnp.where(keep, mk, jnp.zeros_like(mk)), jnp.where(keep, mv, ones_col), sum_lane))
        for i in range(seq // MEM_ROWS):
            rows = pl.ds(i * MEM_ROWS, MEM_ROWS)
            mq = mq_ref[p, rows, :]
            outs = []
            for mk_e, mv_e, sum_lane in heads:
                s = lax.dot_general(mq, mk_e, _NT, preferred_element_type=_F32)
                prob = jnp.exp(s - jnp.max(s, axis=-1, keepdims=True)).astype(_BF16)
                out = jnp.dot(prob, mv_e, preferred_element_type=_F32)
                outs.append(out / out[:, sum_lane:sum_lane + 1])
            o_ref[p, rows, :] = jnp.where(low_half, outs[0], outs[1]).astype(o_ref.dtype)


def _memattn(proj, mq_offset, mem, g_mem, w_kv, batch, seq):
    p4 = proj.reshape(proj.shape[0], batch, seq, LANES)
    mem_len = mem.shape[1]

    def body(mem_ref, g_ref, w_ref, mq_ref, o_ref):
        _memattn_kernel(mem_ref, g_ref, w_ref, mq_ref.at[:, 0], o_ref.at[:, 0], seq=seq)

    out = pl.pallas_call(
        body,
        out_shape=jax.ShapeDtypeStruct((N_MEM_BLOCKS, batch, seq, LANES), _BF16),
        grid=(batch,),
        in_specs=[pl.BlockSpec((1, mem_len, D_MODEL), lambda b: (b, 0, 0)),
                  _const_spec((1, D_MODEL)),
                  _const_spec((D_MODEL, 2 * MEM_WIDTH)),
                  pl.BlockSpec((N_MEM_BLOCKS, 1, seq, LANES), lambda b: (mq_offset // N_MEM_BLOCKS, b, 0, 0))],
        out_specs=pl.BlockSpec((N_MEM_BLOCKS, 1, seq, LANES), lambda b: (0, b, 0, 0)),
        compiler_params=pltpu.CompilerParams(dimension_semantics=("arbitrary",), vmem_limit_bytes=VMEM_LIMIT_BYTES),
        name="memory_attention",
    )(mem, g_mem.reshape(1, D_MODEL), w_kv, p4)
    return out.reshape(N_MEM_BLOCKS, batch * seq, LANES)


def _tail_kernel(x_ref, so_ref, mo_ref, wo_ref, g1_ref, g2_ref, g3_ref, wup_ref, wdn_ref, o_ref):
    mixed = jnp.concatenate([so_ref[j] for j in range(N_SELF_BLOCKS)] + [mo_ref[j] for j in range(N_MEM_BLOCKS)],
                            axis=1)
    y = jnp.dot(mixed, wo_ref[...], preferred_element_type=_F32)
    x1 = x_ref[...] + y * _rms_scale(y) * g1_ref[...]
    h = (x1 * _rms_scale(x1) * g2_ref[...]).astype(_BF16)
    acc = jnp.zeros_like(x1)
    for c in range(D_FF // FF_CHUNK):
        cols = slice(c * FF_CHUNK, (c + 1) * FF_CHUNK)
        u = jnp.maximum(jnp.dot(h, wup_ref[:, cols], preferred_element_type=_F32), 0.0)
        acc = acc + jnp.dot((u * u).astype(_BF16), wdn_ref[cols, :], preferred_element_type=_F32)
    o_ref[...] = x1 + acc * _rms_scale(acc) * g3_ref[...]


def _tail(x2, self_out, mem_out, w_out, g_post_mix, g_pre_mlp, g_post_mlp, w_up, w_down):
    t = x2.shape[0]
    row = lambda: pl.BlockSpec((ROW_TILE, D_MODEL), lambda i: (i, 0))
    gain = lambda g: g.reshape(1, D_MODEL)
    return pl.pallas_call(
        _tail_kernel,
        out_shape=jax.ShapeDtypeStruct((t, D_MODEL), _F32),
        grid=(t // ROW_TILE,),
        in_specs=[row(),
                  pl.BlockSpec((N_SELF_BLOCKS, ROW_TILE, LANES), lambda i: (0, i, 0)),
                  pl.BlockSpec((N_MEM_BLOCKS, ROW_TILE, LANES), lambda i: (0, i, 0)),
                  _const_spec((D_MODEL, D_MODEL)),
                  _const_spec((1, D_MODEL)), _const_spec((1, D_MODEL)), _const_spec((1, D_MODEL)),
                  _const_spec((D_MODEL, D_FF)), _const_spec((D_FF, D_MODEL))],
        out_specs=row(),
        compiler_params=pltpu.CompilerParams(dimension_semantics=("arbitrary",), vmem_limit_bytes=VMEM_LIMIT_BYTES),
        name="outproj_mlp",
    )(x2, self_out, mem_out, w_out, gain(g_post_mix), gain(g_pre_mlp), gain(g_post_mlp), w_up, w_down)


def kernel(x, mem, positions, w_in_ret, w_in_fox, b_forget, w_mem_kv, w_out, w_up, w_down,
           g_pre_mix, g_post_mix, g_pre_mlp, g_post_mlp, g_mem):
    batch, seq, _ = x.shape
    depth = w_out.shape[0]
    assert seq % ROW_TILE == 0 and seq % FOX_TILE == 0 and seq % RET_CHUNK == 0
    x2 = x.reshape(batch * seq, D_MODEL)
    rope = _rope_tables(positions)
    bf = lambda w: w.astype(_BF16)
    for i in range(depth):
        j = i // 2
        if i % 2 == 0:
            epi = ([("rot", 1.0)] * RET_HEADS + [("rot", RET_HEAD_DIM ** -0.5)] * RET_HEADS
                   + [("plain", 1.0)] * (2 * RET_HEADS + N_MEM_BLOCKS))
            (proj,) = _inproj(x2, g_pre_mix[i], bf(w_in_ret[j]), epi, rope=rope)
            self_out = _retention(proj, batch, seq)
            mq_offset = 4 * RET_HEADS
        else:
            w = w_in_fox[j]
            w_f = jnp.zeros((D_MODEL, LANES), w.dtype).at[:, :FOX_HEADS].set(w[:, 3 * FOX_WIDTH:3 * FOX_WIDTH + FOX_HEADS])
            w_cat = jnp.concatenate([w[:, :3 * FOX_WIDTH], w[:, 3 * FOX_WIDTH + FOX_HEADS:], w_f], axis=1)
            npair = FOX_HEADS // 2
            epi = ([("plain", FOX_HEAD_DIM ** -0.5 * LOG2E)] * npair + [("plain", 1.0)] * (2 * npair + N_MEM_BLOCKS))
            proj, flog = _inproj(x2, g_pre_mix[i], bf(w_cat), epi, n_flog=1)
            self_out = _fox(proj, flog, b_forget[j], batch, seq)
            mq_offset = 3 * npair
        mem_out = _memattn(proj, mq_offset, mem, g_mem[i], bf(w_mem_kv[i]), batch, seq)
        x2 = _tail(x2, self_out, mem_out, bf(w_out[i]), g_post_mix[i], g_pre_mlp[i], g_post_mlp[i],
                   bf(w_up[i]), bf(w_down[i]))
    return x2.reshape(batch, seq, D_MODEL)
```

```python
import functools
import math

import jax
import jax.numpy as jnp
from jax import lax
from jax.experimental import pallas as pl
from jax.experimental.pallas import tpu as pltpu

D_MODEL = 1024
MEM_HEADS = 4
MEM_HEAD_DIM = 64
MEM_WIDTH = MEM_HEADS * MEM_HEAD_DIM
RET_HEAD_DIM = 128
RET_HEADS = 6
RET_WIDTH = RET_HEADS * RET_HEAD_DIM
RET_CHUNK = 128
ROPE_BASE = 10000.0
FOX_HEAD_DIM = 64
FOX_HEADS = 12
FOX_WIDTH = FOX_HEADS * FOX_HEAD_DIM
D_FF = 4 * D_MODEL
EPS = 1e-6

LANES = 128
MXU_COLS = 256
VMEM_LIMIT_BYTES = 56 * 1024 * 1024
LOG2E = 1.4426950408889634
NEG_BIG = -1e30

ROW_TILE = 1024
ROW_GROUPS = 2
FF_CHUNK = 512
FOX_TILE = 256
CUM_TILE = 256
MEM_ROWS = 256

N_SELF_BLOCKS = RET_WIDTH // LANES
N_MEM_BLOCKS = MEM_WIDTH // LANES

_NT = (((1,), (1,)), ((), ()))
_F32 = jnp.float32
_BF16 = jnp.bfloat16


def _rms_scale(x):
    return lax.rsqrt(jnp.mean(x * x, axis=-1, keepdims=True) + EPS)


def _const_spec(shape):
    nd = len(shape)
    return pl.BlockSpec(shape, lambda *_: (0,) * nd, pipeline_mode=pl.Buffered(1))


def _rope_kernel(pos_ref, inv_ref, cos_ref, sin_ref):
    r = pos_ref.shape[0]
    ang = pos_ref[...] * inv_ref[...]
    c, s = jnp.cos(ang), jnp.sin(ang)
    half = RET_HEAD_DIM // 2
    c_sw, s_sw = pltpu.roll(c, half, 1), pltpu.roll(s, half, 1)
    low = lax.broadcasted_iota(jnp.int32, (1, LANES), 1) < half
    cos_ref[0:r, :] = jnp.where(low, c, c_sw)
    cos_ref[r:2 * r, :] = jnp.where(low, c_sw, c)
    sin_ref[0:r, :] = jnp.where(low, -s, s_sw)
    sin_ref[r:2 * r, :] = jnp.where(low, -s_sw, s)


def _rope_tables(positions):
    t = positions.size
    half = RET_HEAD_DIM // 2
    inv_freq = ROPE_BASE ** (-jnp.arange(0, RET_HEAD_DIM, 2, dtype=_F32) / RET_HEAD_DIM)
    rows = math.gcd(t // 2, 512)
    pos2 = positions.reshape(t // (2 * rows), 2, rows).transpose(0, 2, 1).reshape(t // 2, 2)
    pos2 = jnp.repeat(pos2.astype(_F32), half, axis=1)
    inv2 = jnp.tile(inv_freq, 2).reshape(1, LANES)
    return pl.pallas_call(
        _rope_kernel,
        out_shape=[jax.ShapeDtypeStruct((t, LANES), _F32)] * 2,
        grid=(t // (2 * rows),),
        in_specs=[pl.BlockSpec((rows, LANES), lambda i: (i, 0)),
                  pl.BlockSpec((1, LANES), lambda i: (0, 0))],
        out_specs=[pl.BlockSpec((2 * rows, LANES), lambda i: (i, 0))] * 2,
        name="rope_tables",
    )(pos2, inv2)


def _inproj_kernel(*refs, epilogues, has_rope, n_flog):
    if has_rope:
        x_ref, g_ref, w_ref, cos_ref, sin_ref = refs[:5]
        rest = refs[5:]
    else:
        x_ref, g_ref, w_ref = refs[:3]
        rest = refs[3:]
    o_ref = rest[0]
    nb = len(epilogues)
    sub = ROW_TILE // ROW_GROUPS
    groups = [pl.ds(part * sub, sub) for part in range(ROW_GROUPS)]

    def pre_norm(rows):
        x = x_ref[rows, :]
        return (x * _rms_scale(x) * g_ref[...]).astype(_BF16)

    def finish(rows, j, a):
        kind, scale = epilogues[j]
        if kind == "rot":
            a = a * cos_ref[rows, :] + pltpu.roll(a, RET_HEAD_DIM // 2, 1) * sin_ref[rows, :]
        if scale != 1.0:
            a = a * scale
        o_ref[j, rows, :] = a.astype(o_ref.dtype)

    hs = [pre_norm(rows) for rows in groups]
    for rows, h in zip(groups, hs):
        for jb in range(nb // 2):
            acc = jnp.dot(h, w_ref[:, jb * MXU_COLS:(jb + 1) * MXU_COLS], preferred_element_type=_F32)
            finish(rows, 2 * jb, acc[:, :LANES])
            finish(rows, 2 * jb + 1, acc[:, LANES:])
        if n_flog:
            base = nb * LANES
            rest[1][rows, :] = jnp.dot(h, w_ref[:, base:base + LANES], preferred_element_type=_F32)


def _inproj(x2, g, w_cat, epilogues, rope=None, n_flog=0):
    t = x2.shape[0]
    nb = len(epilogues)
    ncols = w_cat.shape[1]
    in_specs = [pl.BlockSpec((ROW_TILE, D_MODEL), lambda i: (i, 0)),
                _const_spec((1, D_MODEL)),
                _const_spec((D_MODEL, ncols))]
    args = [x2, g.reshape(1, D_MODEL), w_cat]
    if rope is not None:
        in_specs += [pl.BlockSpec((ROW_TILE, LANES), lambda i: (i, 0))] * 2
        args += list(rope)
    out_shape = [jax.ShapeDtypeStruct((nb, t, LANES), _BF16)]
    out_specs = [pl.BlockSpec((nb, ROW_TILE, LANES), lambda i: (0, i, 0))]
    if n_flog:
        out_shape.append(jax.ShapeDtypeStruct((t, LANES), _F32))
        out_specs.append(pl.BlockSpec((ROW_TILE, LANES), lambda i: (i, 0)))
    return pl.pallas_call(
        functools.partial(_inproj_kernel, epilogues=tuple(epilogues), has_rope=rope is not None, n_flog=n_flog),
        out_shape=out_shape,
        grid=(t // ROW_TILE,),
        in_specs=in_specs,
        out_specs=out_specs,
        compiler_params=pltpu.CompilerParams(dimension_semantics=("arbitrary",), vmem_limit_bytes=VMEM_LIMIT_BYTES),
        name="inproj_rope" if rope is not None else "inproj",
    )(*args)


def _retention_kernel(q_ref, k_ref, v_ref, gate_ref, dmask_ref, qd_ref, kd_ref, cd_ref, o_ref, *, n_chunks):
    c = RET_CHUNK
    dmask = dmask_ref[0]
    qd = qd_ref[0]
    kd = kd_ref[0]
    cd = cd_ref[0]

    inner, kv = [], []
    for n in range(n_chunks):
        rows = pl.ds(n * c, c)
        qc, kc, vc = q_ref[0, rows, :], k_ref[0, rows, :], v_ref[0, rows, :]
        inner.append((lax.dot_general(qc, kc, _NT, preferred_element_type=_F32) * dmask).astype(_BF16))
        k_scaled_t = (kc.astype(_F32) * kd).T.astype(_BF16)
        kv.append(jnp.dot(k_scaled_t, vc, preferred_element_type=_F32))

    state = jnp.zeros((RET_HEAD_DIM, RET_HEAD_DIM), _F32)
    for n in range(n_chunks):
        rows = pl.ds(n * c, c)
        q_scaled = (q_ref[0, rows, :].astype(_F32) * qd).astype(_BF16)
        lhs = jnp.concatenate([inner[n], q_scaled], axis=1)
        rhs = jnp.concatenate([v_ref[0, rows, :], state.astype(_BF16)], axis=0)
        out = jnp.dot(lhs, rhs, preferred_element_type=_F32)
        state = state * cd + kv[n]
        r = out * _rms_scale(out)
        g = gate_ref[0, rows, :].astype(_F32)
        o_ref[0, rows, :] = (r * (g * jax.nn.sigmoid(g))).astype(o_ref.dtype)


def _retention_tables():
    c = RET_CHUNK
    log_gamma = jnp.log(1.0 - 2.0 ** (-5.0 - jnp.arange(RET_HEADS, dtype=_F32)))
    idx = jnp.arange(c, dtype=_F32)
    rel = idx[:, None] - idx[None, :]
    dmask = jnp.where(rel[None] >= 0, jnp.exp(jnp.maximum(rel, 0.0)[None] * log_gamma[:, None, None]), 0.0)
    q_decay = jnp.exp((idx[None, :] + 1.0) * log_gamma[:, None])
    k_decay = jnp.exp((c - 1.0 - idx[None, :]) * log_gamma[:, None])
    chunk_decay = jnp.exp(c * log_gamma)
    full = (RET_HEADS, c, RET_HEAD_DIM)
    return (dmask,
            jnp.broadcast_to(q_decay[:, :, None], full),
            jnp.broadcast_to(k_decay[:, :, None], full),
            jnp.broadcast_to(chunk_decay[:, None, None], full))


def _retention(proj, batch, seq):
    nh = RET_HEADS
    p4 = proj.reshape(proj.shape[0], batch, seq, LANES)

    def slab(offset):
        return pl.BlockSpec((1, 1, seq, LANES), lambda b, h: (offset + h, b, 0, 0))

    def table():
        return pl.BlockSpec((1, RET_CHUNK, RET_HEAD_DIM), lambda b, h: (h, 0, 0))

    def body(q_ref, k_ref, v_ref, gate_ref, dm, qd, kd, cd, o_ref):
        _retention_kernel(q_ref.at[0], k_ref.at[0], v_ref.at[0], gate_ref.at[0], dm, qd, kd, cd, o_ref.at[0],
                          n_chunks=seq // RET_CHUNK)

    out = pl.pallas_call(
        body,
        out_shape=jax.ShapeDtypeStruct((nh, batch, seq, LANES), _BF16),
        grid=(batch, nh),
        in_specs=[slab(0), slab(nh), slab(2 * nh), slab(3 * nh), table(), table(), table(), table()],
        out_specs=pl.BlockSpec((1, 1, seq, LANES), lambda b, h: (h, b, 0, 0)),
        compiler_params=pltpu.CompilerParams(dimension_semantics=("arbitrary", "arbitrary"),
                                             vmem_limit_bytes=VMEM_LIMIT_BYTES),
        name="retention",
    )(p4, p4, p4, p4, *_retention_tables())
    return out.reshape(nh, batch * seq, LANES)


def _split3(x):
    hi = x.astype(_BF16)
    r1 = x - hi.astype(_F32)
    mid = r1.astype(_BF16)
    lo = (r1 - mid.astype(_F32)).astype(_BF16)
    return hi, mid, lo


C_PART_STRIDE = 16
ONES_LANE = LANES - 1


def _fox_kernel(q_ref, k_ref, v_ref, flog_ref, bias_ref, sel_ref, tri_ref, o_ref,
                cpack_ref, qa_ref, ka_ref, va_ref, *, seq):
    pair = pl.program_id(1)
    lane = lax.broadcasted_iota(jnp.int32, (1, LANES), 1)
    low_half = lane < FOX_HEAD_DIM
    keep = (low_half, jnp.logical_not(low_half))
    sum_lane = (FOX_HEAD_DIM, 0)

    @pl.when(pair == 0)
    def _():
        tri = tri_ref[...]
        head_lanes = lane < C_PART_STRIDE
        carry = jnp.zeros((1, LANES), _F32)
        for i in range(seq // CUM_TILE):
            rows = pl.ds(i * CUM_TILE, CUM_TILE)
            z = flog_ref[0, rows, :] + bias_ref[...]
            log_f = jnp.minimum(z, 0.0) - jnp.log1p(jnp.exp(-jnp.abs(z)))
            hi, mid, lo = _split3(log_f)
            cs = (jnp.dot(tri, hi, preferred_element_type=_F32)
                  + jnp.dot(tri, mid, preferred_element_type=_F32)
                  + jnp.dot(tri, lo, preferred_element_type=_F32)) + carry
            carry = cs[CUM_TILE - 1:CUM_TILE, :]
            parts = [jnp.where(head_lanes, part.astype(_F32), 0.0) for part in _split3(cs * LOG2E)]
            packed = (parts[0] + pltpu.roll(parts[1], C_PART_STRIDE, 1) + pltpu.roll(parts[2], 2 * C_PART_STRIDE, 1))
            cpack_ref[rows, :] = jnp.where(lane == ONES_LANE, 1.0, packed).astype(_BF16)

    blk = 512
    for r in range(seq // blk):
        rows = pl.ds(r * blk, blk)
        extras = jnp.dot(cpack_ref[rows, :], sel_ref[0], preferred_element_type=_F32).astype(_BF16)
        for e in range(2):
            qa_ref[e, rows, :] = jnp.where(keep[e], q_ref[0, rows, :], extras[:, (2 * e) * LANES:(2 * e + 1) * LANES])
            ka_ref[e, rows, :] = jnp.where(keep[e], k_ref[0, rows, :], extras[:, (2 * e + 1) * LANES:(2 * e + 2) * LANES])
            ones_col = jnp.where(lane == sum_lane[e], 1.0, 0.0).astype(_BF16)
            va_ref[e, rows, :] = jnp.where(keep[e], v_ref[0, rows, :], ones_col)

    t = FOX_TILE
    row_id = lax.broadcasted_iota(jnp.int32, (t, t), 0)
    col_id = lax.broadcasted_iota(jnp.int32, (t, t), 1)
    causal = col_id <= row_id

    for qi in range(seq // t):
        q_rows = pl.ds(qi * t, t)
        acc = []
        for e in range(2):
            qa = qa_ref[e, q_rows, :]
            m_run = jnp.full((t, 1), NEG_BIG, _F32)
            a = jnp.zeros((t, LANES), _F32)
            for kj in range(qi + 1):
                k_rows = pl.ds(kj * t, t)
                s = lax.dot_general(qa, ka_ref[e, k_rows, :], _NT, preferred_element_type=_F32)
                if kj == qi:
                    s = jnp.where(causal, s, NEG_BIG)
                m_new = jnp.maximum(m_run, jnp.max(s, axis=-1, keepdims=True))
                p = jnp.exp2(s - m_new).astype(_BF16)
                a = a * jnp.exp2(m_run - m_new) + jnp.dot(p, va_ref[e, k_rows, :], preferred_element_type=_F32)
                m_run = m_new
            acc.append(a / a[:, sum_lane[e]:sum_lane[e] + 1])
        o_ref[0, q_rows, :] = jnp.where(low_half, acc[0], acc[1]).astype(o_ref.dtype)


def _fox_selectors():
    import numpy as np
    npair = FOX_HEADS // 2
    sel = np.zeros((npair, LANES, 4 * LANES), np.float32)
    for h in range(FOX_HEADS):
        p, e = divmod(h, 2)
        base = FOX_HEAD_DIM if e == 0 else 0
        qcol, kcol = (2 * e) * LANES + base, (2 * e + 1) * LANES + base
        for part in range(3):
            sel[p, part * C_PART_STRIDE + h, qcol + part] = 1.0
            sel[p, ONES_LANE, kcol + part] = 1.0
            sel[p, ONES_LANE, qcol + 3 + part] = 1.0
            sel[p, part * C_PART_STRIDE + h, kcol + 3 + part] = -1.0
    return jnp.asarray(sel, _BF16)


def _fox(proj, flog, b_forget, batch, seq):
    npair = FOX_HEADS // 2
    p4 = proj.reshape(proj.shape[0], batch, seq, LANES)
    flog3 = flog.reshape(batch, seq, LANES)
    bias = jnp.zeros((1, LANES), _F32).at[0, :FOX_HEADS].set(b_forget.astype(_F32))
    tri = jnp.tril(jnp.ones((CUM_TILE, CUM_TILE), _F32)).astype(_BF16)

    def slab(offset):
        return pl.BlockSpec((1, 1, seq, LANES), lambda b, p: (offset + p, b, 0, 0))

    def body(q_ref, k_ref, v_ref, flog_ref, bias_ref, sel_ref, tri_ref, o_ref, *scratch):
        _fox_kernel(q_ref.at[0], k_ref.at[0], v_ref.at[0], flog_ref, bias_ref, sel_ref, tri_ref,
                    o_ref.at[0], *scratch, seq=seq)

    out = pl.pallas_call(
        body,
        out_shape=jax.ShapeDtypeStruct((npair, batch, seq, LANES), _BF16),
        grid=(batch, npair),
        in_specs=[slab(0), slab(npair), slab(2 * npair),
                  pl.BlockSpec((1, seq, LANES), lambda b, p: (b, 0, 0)),
                  _const_spec((1, LANES)),
                  pl.BlockSpec((1, LANES, 4 * LANES), lambda b, p: (p, 0, 0)),
                  _const_spec((CUM_TILE, CUM_TILE))],
        out_specs=pl.BlockSpec((1, 1, seq, LANES), lambda b, p: (p, b, 0, 0)),
        scratch_shapes=[pltpu.VMEM((seq, LANES), _BF16),
                        pltpu.VMEM((2, seq, LANES), _BF16),
                        pltpu.VMEM((2, seq, LANES), _BF16),
                        pltpu.VMEM((2, seq, LANES), _BF16)],
        compiler_params=pltpu.CompilerParams(dimension_semantics=("arbitrary", "arbitrary"),
                                             vmem_limit_bytes=VMEM_LIMIT_BYTES),
        name="fox_attention",
    )(p4, p4, p4, flog3, bias, _fox_selectors(), tri)
    return out.reshape(npair, batch * seq, LANES)


def _memattn_kernel(mem_ref, g_ref, w_ref, mq_ref, o_ref, *, seq):
    mem = mem_ref[0]
    kvn = (mem * _rms_scale(mem) * g_ref[...]).astype(_BF16)
    kv = jnp.dot(kvn, w_ref[...], preferred_element_type=_F32)
    lane = lax.broadcasted_iota(jnp.int32, (1, LANES), 1)
    low_half = lane < MEM_HEAD_DIM
    scale = MEM_HEAD_DIM ** -0.5
    for p in range(N_MEM_BLOCKS):
        mk = (kv[:, p * LANES:(p + 1) * LANES] * scale).astype(_BF16)
        mv = kv[:, MEM_WIDTH + p * LANES:MEM_WIDTH + (p + 1) * LANES].astype(_BF16)
        heads = []
        for e in range(2):
            keep = low_half if e == 0 else jnp.logical_not(low_half)
            sum_lane = MEM_HEAD_DIM if e == 0 else 0
            ones_col = jnp.where(lane == sum_lane, 1.0, 0.0).astype(_BF16)
            heads.append((jnp.where(keep, mk, jnp.zeros_like(mk)), jnp.where(keep, mv, ones_col), sum_lane))
        for i in range(seq // MEM_ROWS):
            rows = pl.ds(i * MEM_ROWS, MEM_ROWS)
            mq = mq_ref[p, rows, :]
            outs = []
            for mk_e, mv_e, sum_lane in heads:
                s = lax.dot_general(mq, mk_e, _NT, preferred_element_type=_F32)
                prob = jnp.exp(s - jnp.max(s, axis=-1, keepdims=True)).astype(_BF16)
                out = jnp.dot(prob, mv_e, preferred_element_type=_F32)
                outs.append(out / out[:, sum_lane:sum_lane + 1])
            o_ref[p, rows, :] = jnp.where(low_half, outs[0], outs[1]).astype(o_ref.dtype)


def _memattn(proj, mq_offset, mem, g_mem, w_kv, batch, seq):
    p4 = proj.reshape(proj.shape[0], batch, seq, LANES)
    mem_len = mem.shape[1]

    def body(mem_ref, g_ref, w_ref, mq_ref, o_ref):
        _memattn_kernel(mem_ref, g_ref, w_ref, mq_ref.at[:, 0], o_ref.at[:, 0], seq=seq)

    out = pl.pallas_call(
        body,
        out_shape=jax.ShapeDtypeStruct((N_MEM_BLOCKS, batch, seq, LANES), _BF16),
        grid=(batch,),
        in_specs=[pl.BlockSpec((1, mem_len, D_MODEL), lambda b: (b, 0, 0)),
                  _const_spec((1, D_MODEL)),
                  _const_spec((D_MODEL, 2 * MEM_WIDTH)),
                  pl.BlockSpec((N_MEM_BLOCKS, 1, seq, LANES), lambda b: (mq_offset // N_MEM_BLOCKS, b, 0, 0))],
        out_specs=pl.BlockSpec((N_MEM_BLOCKS, 1, seq, LANES), lambda b: (0, b, 0, 0)),
        compiler_params=pltpu.CompilerParams(dimension_semantics=("arbitrary",), vmem_limit_bytes=VMEM_LIMIT_BYTES),
        name="memory_attention",
    )(mem, g_mem.reshape(1, D_MODEL), w_kv, p4)
    return out.reshape(N_MEM_BLOCKS, batch * seq, LANES)


def _tail_kernel(x_ref, so_ref, mo_ref, wo_ref, g1_ref, g2_ref, g3_ref, wup_ref, wdn_ref, o_ref):
    sub = ROW_TILE // ROW_GROUPS
    groups = [pl.ds(part * sub, sub) for part in range(ROW_GROUPS)]

    def out_proj(rows):
        mixed = jnp.concatenate([so_ref[j, rows, :] for j in range(N_SELF_BLOCKS)]
                                + [mo_ref[j, rows, :] for j in range(N_MEM_BLOCKS)], axis=1)
        return jnp.dot(mixed, wo_ref[...], preferred_element_type=_F32)

    def mid_norms(rows, y):
        x1 = x_ref[rows, :] + y * _rms_scale(y) * g1_ref[...]
        return x1, (x1 * _rms_scale(x1) * g2_ref[...]).astype(_BF16)

    def mlp(h):
        acc = jnp.zeros((sub, D_MODEL), _F32)
        for c in range(D_FF // FF_CHUNK):
            cols = slice(c * FF_CHUNK, (c + 1) * FF_CHUNK)
            u = jnp.maximum(jnp.dot(h, wup_ref[:, cols], preferred_element_type=_F32), 0.0)
            acc = acc + jnp.dot((u * u).astype(_BF16), wdn_ref[cols, :], preferred_element_type=_F32)
        return acc

    ys = [out_proj(rows) for rows in groups]
    mids = [mid_norms(rows, y) for rows, y in zip(groups, ys)]
    for rows, (x1, h) in zip(groups, mids):
        acc = mlp(h)
        o_ref[rows, :] = x1 + acc * _rms_scale(acc) * g3_ref[...]


def _tail(x2, self_out, mem_out, w_out, g_post_mix, g_pre_mlp, g_post_mlp, w_up, w_down):
    t = x2.shape[0]
    row = lambda: pl.BlockSpec((ROW_TILE, D_MODEL), lambda i: (i, 0))
    gain = lambda g: g.reshape(1, D_MODEL)
    return pl.pallas_call(
        _tail_kernel,
        out_shape=jax.ShapeDtypeStruct((t, D_MODEL), _F32),
        grid=(t // ROW_TILE,),
        in_specs=[row(),
                  pl.BlockSpec((N_SELF_BLOCKS, ROW_TILE, LANES), lambda i: (0, i, 0)),
                  pl.BlockSpec((N_MEM_BLOCKS, ROW_TILE, LANES), lambda i: (0, i, 0)),
                  _const_spec((D_MODEL, D_MODEL)),
                  _const_spec((1, D_MODEL)), _const_spec((1, D_MODEL)), _const_spec((1, D_MODEL)),
                  _const_spec((D_MODEL, D_FF)), _const_spec((D_FF, D_MODEL))],
        out_specs=row(),
        compiler_params=pltpu.CompilerParams(dimension_semantics=("arbitrary",), vmem_limit_bytes=VMEM_LIMIT_BYTES),
        name="outproj_mlp",
    )(x2, self_out, mem_out, w_out, gain(g_post_mix), gain(g_pre_mlp), gain(g_post_mlp), w_up, w_down)


def kernel(x, mem, positions, w_in_ret, w_in_fox, b_forget, w_mem_kv, w_out, w_up, w_down,
           g_pre_mix, g_post_mix, g_pre_mlp, g_post_mlp, g_mem):
    batch, seq, _ = x.shape
    depth = w_out.shape[0]
    assert (batch * seq) % ROW_TILE == 0 and seq % FOX_TILE == 0 and seq % RET_CHUNK == 0
    x2 = x.reshape(batch * seq, D_MODEL)
    rope = _rope_tables(positions)
    bf = lambda w: w.astype(_BF16)
    for i in range(depth):
        j = i // 2
        if i % 2 == 0:
            epi = ([("rot", 1.0)] * RET_HEADS + [("rot", RET_HEAD_DIM ** -0.5)] * RET_HEADS
                   + [("plain", 1.0)] * (2 * RET_HEADS + N_MEM_BLOCKS))
            (proj,) = _inproj(x2, g_pre_mix[i], bf(w_in_ret[j]), epi, rope=rope)
            self_out = _retention(proj, batch, seq)
            mq_offset = 4 * RET_HEADS
        else:
            w = bf(w_in_fox[j])
            w_f = jnp.pad(w[:, 3 * FOX_WIDTH:3 * FOX_WIDTH + FOX_HEADS], ((0, 0), (0, LANES - FOX_HEADS)))
            w_cat = jnp.concatenate([w[:, :3 * FOX_WIDTH], w[:, 3 * FOX_WIDTH + FOX_HEADS:], w_f], axis=1)
            npair = FOX_HEADS // 2
            epi = ([("plain", FOX_HEAD_DIM ** -0.5 * LOG2E)] * npair + [("plain", 1.0)] * (2 * npair + N_MEM_BLOCKS))
            proj, flog = _inproj(x2, g_pre_mix[i], bf(w_cat), epi, n_flog=1)
            self_out = _fox(proj, flog, b_forget[j], batch, seq)
            mq_offset = 3 * npair
        mem_out = _memattn(proj, mq_offset, mem, g_mem[i], bf(w_mem_kv[i]), batch, seq)
        x2 = _tail(x2, self_out, mem_out, bf(w_out[i]), g_post_mix[i], g_pre_mlp[i], g_post_mlp[i],
                   bf(w_up[i]), bf(w_down[i]))
    return x2.reshape(batch, seq, D_MODEL)
```

```python
import functools
import math

import jax
import jax.numpy as jnp
from jax import lax
from jax.experimental import pallas as pl
from jax.experimental.pallas import tpu as pltpu

D_MODEL = 1024
MEM_HEADS = 4
MEM_HEAD_DIM = 64
MEM_WIDTH = MEM_HEADS * MEM_HEAD_DIM
RET_HEAD_DIM = 128
RET_HEADS = 6
RET_WIDTH = RET_HEADS * RET_HEAD_DIM
RET_CHUNK = 128
ROPE_BASE = 10000.0
FOX_HEAD_DIM = 64
FOX_HEADS = 12
FOX_WIDTH = FOX_HEADS * FOX_HEAD_DIM
D_FF = 4 * D_MODEL
EPS = 1e-6

LANES = 128
MXU_COLS = 256
VMEM_LIMIT_BYTES = 56 * 1024 * 1024
LOG2E = 1.4426950408889634
NEG_BIG = -1e30

ROW_TILE = 1024
ROW_GROUPS = 2
FF_CHUNK = 512
FOX_TILE = 256
CUM_TILE = 256
MEM_ROWS = 256

N_SELF_BLOCKS = RET_WIDTH // LANES
N_MEM_BLOCKS = MEM_WIDTH // LANES

_NT = (((1,), (1,)), ((), ()))
_F32 = jnp.float32
_BF16 = jnp.bfloat16


def _rms_scale(x):
    return lax.rsqrt(jnp.mean(x * x, axis=-1, keepdims=True) + EPS)


def _const_spec(shape):
    nd = len(shape)
    return pl.BlockSpec(shape, lambda *_: (0,) * nd, pipeline_mode=pl.Buffered(1))


def _rope_kernel(pos_ref, inv_ref, cos_ref, sin_ref):
    r = pos_ref.shape[0]
    ang = pos_ref[...] * inv_ref[...]
    c, s = jnp.cos(ang), jnp.sin(ang)
    half = RET_HEAD_DIM // 2
    c_sw, s_sw = pltpu.roll(c, half, 1), pltpu.roll(s, half, 1)
    low = lax.broadcasted_iota(jnp.int32, (1, LANES), 1) < half
    cos_ref[0:r, :] = jnp.where(low, c, c_sw)
    cos_ref[r:2 * r, :] = jnp.where(low, c_sw, c)
    sin_ref[0:r, :] = jnp.where(low, -s, s_sw)
    sin_ref[r:2 * r, :] = jnp.where(low, -s_sw, s)


def _rope_tables(positions):
    t = positions.size
    half = RET_HEAD_DIM // 2
    inv_freq = ROPE_BASE ** (-jnp.arange(0, RET_HEAD_DIM, 2, dtype=_F32) / RET_HEAD_DIM)
    rows = math.gcd(t // 2, 512)
    pos2 = positions.reshape(t // (2 * rows), 2, rows).transpose(0, 2, 1).reshape(t // 2, 2)
    pos2 = jnp.repeat(pos2.astype(_F32), half, axis=1)
    inv2 = jnp.tile(inv_freq, 2).reshape(1, LANES)
    return pl.pallas_call(
        _rope_kernel,
        out_shape=[jax.ShapeDtypeStruct((t, LANES), _F32)] * 2,
        grid=(t // (2 * rows),),
        in_specs=[pl.BlockSpec((rows, LANES), lambda i: (i, 0)),
                  pl.BlockSpec((1, LANES), lambda i: (0, 0))],
        out_specs=[pl.BlockSpec((2 * rows, LANES), lambda i: (i, 0))] * 2,
        name="rope_tables",
    )(pos2, inv2)


def _inproj_kernel(*refs, epilogues, has_rope, n_flog):
    if has_rope:
        x_ref, g_ref, w_ref, cos_ref, sin_ref = refs[:5]
        rest = refs[5:]
    else:
        x_ref, g_ref, w_ref = refs[:3]
        rest = refs[3:]
    o_ref = rest[0]
    nb = len(epilogues)
    sub = ROW_TILE // ROW_GROUPS
    groups = [pl.ds(part * sub, sub) for part in range(ROW_GROUPS)]

    def pre_norm(rows):
        x = x_ref[rows, :]
        return (x * _rms_scale(x) * g_ref[...]).astype(_BF16)

    def finish(rows, j, a):
        kind, scale = epilogues[j]
        if kind == "rot":
            a = a * cos_ref[rows, :] + pltpu.roll(a, RET_HEAD_DIM // 2, 1) * sin_ref[rows, :]
        if scale != 1.0:
            a = a * scale
        o_ref[j, rows, :] = a.astype(o_ref.dtype)

    hs = [pre_norm(rows) for rows in groups]
    for rows, h in zip(groups, hs):
        for jb in range(nb // 2):
            acc = jnp.dot(h, w_ref[:, jb * MXU_COLS:(jb + 1) * MXU_COLS], preferred_element_type=_F32)
            finish(rows, 2 * jb, acc[:, :LANES])
            finish(rows, 2 * jb + 1, acc[:, LANES:])
        if n_flog:
            base = nb * LANES
            rest[1][rows, :] = jnp.dot(h, w_ref[:, base:base + LANES], preferred_element_type=_F32)


def _inproj(x2, g, w_cat, epilogues, rope=None, n_flog=0):
    t = x2.shape[0]
    nb = len(epilogues)
    ncols = w_cat.shape[1]
    in_specs = [pl.BlockSpec((ROW_TILE, D_MODEL), lambda i: (i, 0)),
                _const_spec((1, D_MODEL)),
                _const_spec((D_MODEL, ncols))]
    args = [x2, g.reshape(1, D_MODEL), w_cat]
    if rope is not None:
        in_specs += [pl.BlockSpec((ROW_TILE, LANES), lambda i: (i, 0))] * 2
        args += list(rope)
    out_shape = [jax.ShapeDtypeStruct((nb, t, LANES), _BF16)]
    out_specs = [pl.BlockSpec((nb, ROW_TILE, LANES), lambda i: (0, i, 0))]
    if n_flog:
        out_shape.append(jax.ShapeDtypeStruct((t, LANES), _F32))
        out_specs.append(pl.BlockSpec((ROW_TILE, LANES), lambda i: (i, 0)))
    return pl.pallas_call(
        functools.partial(_inproj_kernel, epilogues=tuple(epilogues), has_rope=rope is not None, n_flog=n_flog),
        out_shape=out_shape,
        grid=(t // ROW_TILE,),
        in_specs=in_specs,
        out_specs=out_specs,
        compiler_params=pltpu.CompilerParams(dimension_semantics=("arbitrary",), vmem_limit_bytes=VMEM_LIMIT_BYTES),
        name="inproj_rope" if rope is not None else "inproj",
    )(*args)


def _retention_kernel(q_ref, k_ref, v_ref, gate_ref, dmask_ref, qd_ref, kd_ref, cd_ref, o_ref, *, n_chunks):
    c = RET_CHUNK
    dmask = dmask_ref[0]
    qd = qd_ref[0]
    kd = kd_ref[0]
    cd = cd_ref[0]

    inner, kv = [], []
    for n in range(n_chunks):
        rows = pl.ds(n * c, c)
        qc, kc, vc = q_ref[0, rows, :], k_ref[0, rows, :], v_ref[0, rows, :]
        inner.append((lax.dot_general(qc, kc, _NT, preferred_element_type=_F32) * dmask).astype(_BF16))
        k_scaled_t = (kc.astype(_F32) * kd).T.astype(_BF16)
        kv.append(jnp.dot(k_scaled_t, vc, preferred_element_type=_F32))

    state = jnp.zeros((RET_HEAD_DIM, RET_HEAD_DIM), _F32)
    for n in range(n_chunks):
        rows = pl.ds(n * c, c)
        q_scaled = (q_ref[0, rows, :].astype(_F32) * qd).astype(_BF16)
        lhs = jnp.concatenate([inner[n], q_scaled], axis=1)
        rhs = jnp.concatenate([v_ref[0, rows, :], state.astype(_BF16)], axis=0)
        out = jnp.dot(lhs, rhs, preferred_element_type=_F32)
        state = state * cd + kv[n]
        r = out * _rms_scale(out)
        g = gate_ref[0, rows, :].astype(_F32)
        o_ref[0, rows, :] = (r * (g * jax.nn.sigmoid(g))).astype(o_ref.dtype)


def _retention_tables():
    c = RET_CHUNK
    log_gamma = jnp.log(1.0 - 2.0 ** (-5.0 - jnp.arange(RET_HEADS, dtype=_F32)))
    idx = jnp.arange(c, dtype=_F32)
    rel = idx[:, None] - idx[None, :]
    dmask = jnp.where(rel[None] >= 0, jnp.exp(jnp.maximum(rel, 0.0)[None] * log_gamma[:, None, None]), 0.0)
    q_decay = jnp.exp((idx[None, :] + 1.0) * log_gamma[:, None])
    k_decay = jnp.exp((c - 1.0 - idx[None, :]) * log_gamma[:, None])
    chunk_decay = jnp.exp(c * log_gamma)
    full = (RET_HEADS, c, RET_HEAD_DIM)
    return (dmask,
            jnp.broadcast_to(q_decay[:, :, None], full),
            jnp.broadcast_to(k_decay[:, :, None], full),
            jnp.broadcast_to(chunk_decay[:, None, None], full))


def _retention(proj, batch, seq):
    nh = RET_HEADS
    p4 = proj.reshape(proj.shape[0], batch, seq, LANES)

    def slab(offset):
        return pl.BlockSpec((1, 1, seq, LANES), lambda b, h: (offset + h, b, 0, 0))

    def table():
        return pl.BlockSpec((1, RET_CHUNK, RET_HEAD_DIM), lambda b, h: (h, 0, 0))

    def body(q_ref, k_ref, v_ref, gate_ref, dm, qd, kd, cd, o_ref):
        _retention_kernel(q_ref.at[0], k_ref.at[0], v_ref.at[0], gate_ref.at[0], dm, qd, kd, cd, o_ref.at[0],
                          n_chunks=seq // RET_CHUNK)

    out = pl.pallas_call(
        body,
        out_shape=jax.ShapeDtypeStruct((nh, batch, seq, LANES), _BF16),
        grid=(batch, nh),
        in_specs=[slab(0), slab(nh), slab(2 * nh), slab(3 * nh), table(), table(), table(), table()],
        out_specs=pl.BlockSpec((1, 1, seq, LANES), lambda b, h: (h, b, 0, 0)),
        compiler_params=pltpu.CompilerParams(dimension_semantics=("arbitrary", "arbitrary"),
                                             vmem_limit_bytes=VMEM_LIMIT_BYTES),
        name="retention",
    )(p4, p4, p4, p4, *_retention_tables())
    return out.reshape(nh, batch * seq, LANES)


def _split3(x):
    hi = x.astype(_BF16)
    r1 = x - hi.astype(_F32)
    mid = r1.astype(_BF16)
    lo = (r1 - mid.astype(_F32)).astype(_BF16)
    return hi, mid, lo


C_PART_STRIDE = 16
ONES_LANE = LANES - 1


def _fox_kernel(q_ref, k_ref, v_ref, flog_ref, bias_ref, sel_ref, tri_ref, o_ref,
                cpack_ref, qa_ref, ka_ref, va_ref, *, seq):
    pair = pl.program_id(1)
    lane = lax.broadcasted_iota(jnp.int32, (1, LANES), 1)
    low_half = lane < FOX_HEAD_DIM
    keep = (low_half, jnp.logical_not(low_half))
    sum_lane = (FOX_HEAD_DIM, 0)

    @pl.when(pair == 0)
    def _():
        tri = tri_ref[...]
        head_lanes = lane < C_PART_STRIDE
        carry = jnp.zeros((1, LANES), _F32)
        for i in range(seq // CUM_TILE):
            rows = pl.ds(i * CUM_TILE, CUM_TILE)
            z = flog_ref[0, rows, :] + bias_ref[...]
            log_f = jnp.minimum(z, 0.0) - jnp.log1p(jnp.exp(-jnp.abs(z)))
            hi, mid, lo = _split3(log_f)
            cs = (jnp.dot(tri, hi, preferred_element_type=_F32)
                  + jnp.dot(tri, mid, preferred_element_type=_F32)
                  + jnp.dot(tri, lo, preferred_element_type=_F32)) + carry
            carry = cs[CUM_TILE - 1:CUM_TILE, :]
            parts = [jnp.where(head_lanes, part.astype(_F32), 0.0) for part in _split3(cs * LOG2E)]
            packed = (parts[0] + pltpu.roll(parts[1], C_PART_STRIDE, 1) + pltpu.roll(parts[2], 2 * C_PART_STRIDE, 1))
            cpack_ref[rows, :] = jnp.where(lane == ONES_LANE, 1.0, packed).astype(_BF16)

    blk = 512
    for r in range(seq // blk):
        rows = pl.ds(r * blk, blk)
        extras = jnp.dot(cpack_ref[rows, :], sel_ref[0], preferred_element_type=_F32).astype(_BF16)
        for e in range(2):
            qa_ref[e, rows, :] = jnp.where(keep[e], q_ref[0, rows, :], extras[:, (2 * e) * LANES:(2 * e + 1) * LANES])
            ka_ref[e, rows, :] = jnp.where(keep[e], k_ref[0, rows, :], extras[:, (2 * e + 1) * LANES:(2 * e + 2) * LANES])
            ones_col = jnp.where(lane == sum_lane[e], 1.0, 0.0).astype(_BF16)
            va_ref[e, rows, :] = jnp.where(keep[e], v_ref[0, rows, :], ones_col)

    t = FOX_TILE
    row_id = lax.broadcasted_iota(jnp.int32, (t, t), 0)
    col_id = lax.broadcasted_iota(jnp.int32, (t, t), 1)
    causal = col_id <= row_id

    for qi in range(seq // t):
        q_rows = pl.ds(qi * t, t)
        acc = []
        for e in range(2):
            qa = qa_ref[e, q_rows, :]
            m_run = jnp.full((t, 1), NEG_BIG, _F32)
            a = jnp.zeros((t, LANES), _F32)
            for kj in range(qi + 1):
                k_rows = pl.ds(kj * t, t)
                s = lax.dot_general(qa, ka_ref[e, k_rows, :], _NT, preferred_element_type=_F32)
                if kj == qi:
                    s = jnp.where(causal, s, NEG_BIG)
                m_new = jnp.maximum(m_run, jnp.max(s, axis=-1, keepdims=True))
                p = jnp.exp2(s - m_new).astype(_BF16)
                a = a * jnp.exp2(m_run - m_new) + jnp.dot(p, va_ref[e, k_rows, :], preferred_element_type=_F32)
                m_run = m_new
            acc.append(a / a[:, sum_lane[e]:sum_lane[e] + 1])
        o_ref[0, q_rows, :] = jnp.where(low_half, acc[0], acc[1]).astype(o_ref.dtype)


def _fox_selectors():
    import numpy as np
    npair = FOX_HEADS // 2
    sel = np.zeros((npair, LANES, 4 * LANES), np.float32)
    for h in range(FOX_HEADS):
        p, e = divmod(h, 2)
        base = FOX_HEAD_DIM if e == 0 else 0
        qcol, kcol = (2 * e) * LANES + base, (2 * e + 1) * LANES + base
        for part in range(3):
            sel[p, part * C_PART_STRIDE + h, qcol + part] = 1.0
            sel[p, ONES_LANE, kcol + part] = 1.0
            sel[p, ONES_LANE, qcol + 3 + part] = 1.0
            sel[p, part * C_PART_STRIDE + h, kcol + 3 + part] = -1.0
    return jnp.asarray(sel, _BF16)


def _fox(proj, flog, b_forget, batch, seq):
    npair = FOX_HEADS // 2
    p4 = proj.reshape(proj.shape[0], batch, seq, LANES)
    flog3 = flog.reshape(batch, seq, LANES)
    bias = jnp.zeros((1, LANES), _F32).at[0, :FOX_HEADS].set(b_forget.astype(_F32))
    tri = jnp.tril(jnp.ones((CUM_TILE, CUM_TILE), _F32)).astype(_BF16)

    def slab(offset):
        return pl.BlockSpec((1, 1, seq, LANES), lambda b, p: (offset + p, b, 0, 0))

    def body(q_ref, k_ref, v_ref, flog_ref, bias_ref, sel_ref, tri_ref, o_ref, *scratch):
        _fox_kernel(q_ref.at[0], k_ref.at[0], v_ref.at[0], flog_ref, bias_ref, sel_ref, tri_ref,
                    o_ref.at[0], *scratch, seq=seq)

    out = pl.pallas_call(
        body,
        out_shape=jax.ShapeDtypeStruct((npair, batch, seq, LANES), _BF16),
        grid=(batch, npair),
        in_specs=[slab(0), slab(npair), slab(2 * npair),
                  pl.BlockSpec((1, seq, LANES), lambda b, p: (b, 0, 0)),
                  _const_spec((1, LANES)),
                  pl.BlockSpec((1, LANES, 4 * LANES), lambda b, p: (p, 0, 0)),
                  _const_spec((CUM_TILE, CUM_TILE))],
        out_specs=pl.BlockSpec((1, 1, seq, LANES), lambda b, p: (p, b, 0, 0)),
        scratch_shapes=[pltpu.VMEM((seq, LANES), _BF16),
                        pltpu.VMEM((2, seq, LANES), _BF16),
                        pltpu.VMEM((2, seq, LANES), _BF16),
                        pltpu.VMEM((2, seq, LANES), _BF16)],
        compiler_params=pltpu.CompilerParams(dimension_semantics=("arbitrary", "arbitrary"),
                                             vmem_limit_bytes=VMEM_LIMIT_BYTES),
        name="fox_attention",
    )(p4, p4, p4, flog3, bias, _fox_selectors(), tri)
    return out.reshape(npair, batch * seq, LANES)


def _memattn_kernel(mem_ref, g_ref, w_ref, mq_ref, o_ref, *, seq):
    mem = mem_ref[0]
    kvn = (mem * _rms_scale(mem) * g_ref[...]).astype(_BF16)
    kv = jnp.dot(kvn, w_ref[...], preferred_element_type=_F32)
    lane = lax.broadcasted_iota(jnp.int32, (1, LANES), 1)
    low_half = lane < MEM_HEAD_DIM
    scale = MEM_HEAD_DIM ** -0.5
    for p in range(N_MEM_BLOCKS):
        mk = (kv[:, p * LANES:(p + 1) * LANES] * scale).astype(_BF16)
        mv = kv[:, MEM_WIDTH + p * LANES:MEM_WIDTH + (p + 1) * LANES].astype(_BF16)
        heads = []
        for e in range(2):
            keep = low_half if e == 0 else jnp.logical_not(low_half)
            sum_lane = MEM_HEAD_DIM if e == 0 else 0
            ones_col = jnp.where(lane == sum_lane, 1.0, 0.0).astype(_BF16)
            heads.append((jnp.where(keep, mk, jnp.zeros_like(mk)), jnp.where(keep, mv, ones_col), sum_lane))
        row_tiles = [pl.ds(i * MEM_ROWS, MEM_ROWS) for i in range(seq // MEM_ROWS)]
        scores = [[lax.dot_general(mq_ref[p, rows, :], mk_e, _NT, preferred_element_type=_F32)
                   for mk_e, _, _ in heads] for rows in row_tiles]
        probs = [[jnp.exp(s - jnp.max(s, axis=-1, keepdims=True)).astype(_BF16) for s in pair] for pair in scores]
        for rows, pair in zip(row_tiles, probs):
            outs = []
            for prob, (_, mv_e, sum_lane) in zip(pair, heads):
                out = jnp.dot(prob, mv_e, preferred_element_type=_F32)
                outs.append(out / out[:, sum_lane:sum_lane + 1])
            o_ref[p, rows, :] = jnp.where(low_half, outs[0], outs[1]).astype(o_ref.dtype)


def _memattn(proj, mq_offset, mem, g_mem, w_kv, batch, seq):
    p4 = proj.reshape(proj.shape[0], batch, seq, LANES)
    mem_len = mem.shape[1]

    def body(mem_ref, g_ref, w_ref, mq_ref, o_ref):
        _memattn_kernel(mem_ref, g_ref, w_ref, mq_ref.at[:, 0], o_ref.at[:, 0], seq=seq)

    out = pl.pallas_call(
        body,
        out_shape=jax.ShapeDtypeStruct((N_MEM_BLOCKS, batch, seq, LANES), _BF16),
        grid=(batch,),
        in_specs=[pl.BlockSpec((1, mem_len, D_MODEL), lambda b: (b, 0, 0)),
                  _const_spec((1, D_MODEL)),
                  _const_spec((D_MODEL, 2 * MEM_WIDTH)),
                  pl.BlockSpec((N_MEM_BLOCKS, 1, seq, LANES), lambda b: (mq_offset // N_MEM_BLOCKS, b, 0, 0))],
        out_specs=pl.BlockSpec((N_MEM_BLOCKS, 1, seq, LANES), lambda b: (0, b, 0, 0)),
        compiler_params=pltpu.CompilerParams(dimension_semantics=("arbitrary",), vmem_limit_bytes=VMEM_LIMIT_BYTES),
        name="memory_attention",
    )(mem, g_mem.reshape(1, D_MODEL), w_kv, p4)
    return out.reshape(N_MEM_BLOCKS, batch * seq, LANES)


def _tail_kernel(x_ref, so_ref, mo_ref, wo_ref, g1_ref, g2_ref, g3_ref, wup_ref, wdn_ref, o_ref):
    sub = ROW_TILE // ROW_GROUPS
    groups = [pl.ds(part * sub, sub) for part in range(ROW_GROUPS)]

    def out_proj(rows):
        mixed = jnp.concatenate([so_ref[j, rows, :] for j in range(N_SELF_BLOCKS)]
                                + [mo_ref[j, rows, :] for j in range(N_MEM_BLOCKS)], axis=1)
        return jnp.dot(mixed, wo_ref[...], preferred_element_type=_F32)

    def mid_norms(rows, y):
        x1 = x_ref[rows, :] + y * _rms_scale(y) * g1_ref[...]
        return x1, (x1 * _rms_scale(x1) * g2_ref[...]).astype(_BF16)

    def mlp(h):
        acc = jnp.zeros((sub, D_MODEL), _F32)
        for c in range(D_FF // FF_CHUNK):
            cols = slice(c * FF_CHUNK, (c + 1) * FF_CHUNK)
            u = jnp.maximum(jnp.dot(h, wup_ref[:, cols], preferred_element_type=_F32), 0.0)
            acc = acc + jnp.dot((u * u).astype(_BF16), wdn_ref[cols, :], preferred_element_type=_F32)
        return acc

    ys = [out_proj(rows) for rows in groups]
    mids = [mid_norms(rows, y) for rows, y in zip(groups, ys)]
    for rows, (x1, h) in zip(groups, mids):
        acc = mlp(h)
        o_ref[rows, :] = x1 + acc * _rms_scale(acc) * g3_ref[...]


def _tail(x2, self_out, mem_out, w_out, g_post_mix, g_pre_mlp, g_post_mlp, w_up, w_down):
    t = x2.shape[0]
    row = lambda: pl.BlockSpec((ROW_TILE, D_MODEL), lambda i: (i, 0))
    gain = lambda g: g.reshape(1, D_MODEL)
    return pl.pallas_call(
        _tail_kernel,
        out_shape=jax.ShapeDtypeStruct((t, D_MODEL), _F32),
        grid=(t // ROW_TILE,),
        in_specs=[row(),
                  pl.BlockSpec((N_SELF_BLOCKS, ROW_TILE, LANES), lambda i: (0, i, 0)),
                  pl.BlockSpec((N_MEM_BLOCKS, ROW_TILE, LANES), lambda i: (0, i, 0)),
                  _const_spec((D_MODEL, D_MODEL)),
                  _const_spec((1, D_MODEL)), _const_spec((1, D_MODEL)), _const_spec((1, D_MODEL)),
                  _const_spec((D_MODEL, D_FF)), _const_spec((D_FF, D_MODEL))],
        out_specs=row(),
        compiler_params=pltpu.CompilerParams(dimension_semantics=("arbitrary",), vmem_limit_bytes=VMEM_LIMIT_BYTES),
        name="outproj_mlp",
    )(x2, self_out, mem_out, w_out, gain(g_post_mix), gain(g_pre_mlp), gain(g_post_mlp), w_up, w_down)


def kernel(x, mem, positions, w_in_ret, w_in_fox, b_forget, w_mem_kv, w_out, w_up, w_down,
           g_pre_mix, g_post_mix, g_pre_mlp, g_post_mlp, g_mem):
    batch, seq, _ = x.shape
    depth = w_out.shape[0]
    assert (batch * seq) % ROW_TILE == 0 and seq % FOX_TILE == 0 and seq % RET_CHUNK == 0
    x2 = x.reshape(batch * seq, D_MODEL)
    rope = _rope_tables(positions)
    bf = lambda w: w.astype(_BF16)
    for i in range(depth):
        j = i // 2
        if i % 2 == 0:
            epi = ([("rot", 1.0)] * RET_HEADS + [("rot", RET_HEAD_DIM ** -0.5)] * RET_HEADS
                   + [("plain", 1.0)] * (2 * RET_HEADS + N_MEM_BLOCKS))
            (proj,) = _inproj(x2, g_pre_mix[i], bf(w_in_ret[j]), epi, rope=rope)
            self_out = _retention(proj, batch, seq)
            mq_offset = 4 * RET_HEADS
        else:
            w = bf(w_in_fox[j])
            w_f = jnp.pad(w[:, 3 * FOX_WIDTH:3 * FOX_WIDTH + FOX_HEADS], ((0, 0), (0, LANES - FOX_HEADS)))
            w_cat = jnp.concatenate([w[:, :3 * FOX_WIDTH], w[:, 3 * FOX_WIDTH + FOX_HEADS:], w_f], axis=1)
            npair = FOX_HEADS // 2
            epi = ([("plain", FOX_HEAD_DIM ** -0.5 * LOG2E)] * npair + [("plain", 1.0)] * (2 * npair + N_MEM_BLOCKS))
            proj, flog = _inproj(x2, g_pre_mix[i], bf(w_cat), epi, n_flog=1)
            self_out = _fox(proj, flog, b_forget[j], batch, seq)
            mq_offset = 3 * npair
        mem_out = _memattn(proj, mq_offset, mem, g_mem[i], bf(w_mem_kv[i]), batch, seq)
        x2 = _tail(x2, self_out, mem_out, bf(w_out[i]), g_post_mix[i], g_pre_mlp[i], g_post_mlp[i],
                   bf(w_up[i]), bf(w_down[i]))
    return x2.reshape(batch, seq, D_MODEL)
```

```python
import functools
import math

import jax
import jax.numpy as jnp
from jax import lax
from jax.experimental import pallas as pl
from jax.experimental.pallas import tpu as pltpu

D_MODEL = 1024
MEM_HEADS = 4
MEM_HEAD_DIM = 64
MEM_WIDTH = MEM_HEADS * MEM_HEAD_DIM
RET_HEAD_DIM = 128
RET_HEADS = 6
RET_WIDTH = RET_HEADS * RET_HEAD_DIM
RET_CHUNK = 128
ROPE_BASE = 10000.0
FOX_HEAD_DIM = 64
FOX_HEADS = 12
FOX_WIDTH = FOX_HEADS * FOX_HEAD_DIM
D_FF = 4 * D_MODEL
EPS = 1e-6

LANES = 128
MXU_COLS = 256
VMEM_LIMIT_BYTES = 56 * 1024 * 1024
LOG2E = 1.4426950408889634
NEG_BIG = -1e30

ROW_TILE = 1024
ROW_GROUPS = 2
FF_CHUNK = 512
FOX_TILE = 256
CUM_TILE = 256
MEM_ROWS = 256

N_SELF_BLOCKS = RET_WIDTH // LANES
N_MEM_BLOCKS = MEM_WIDTH // LANES

_NT = (((1,), (1,)), ((), ()))
_F32 = jnp.float32
_BF16 = jnp.bfloat16


def _rms_scale(x):
    return lax.rsqrt(jnp.mean(x * x, axis=-1, keepdims=True) + EPS)


def _const_spec(shape):
    nd = len(shape)
    return pl.BlockSpec(shape, lambda *_: (0,) * nd, pipeline_mode=pl.Buffered(1))


def _rope_kernel(pos_ref, inv_ref, cos_ref, sin_ref):
    r = pos_ref.shape[0]
    ang = pos_ref[...] * inv_ref[...]
    c, s = jnp.cos(ang), jnp.sin(ang)
    half = RET_HEAD_DIM // 2
    c_sw, s_sw = pltpu.roll(c, half, 1), pltpu.roll(s, half, 1)
    low = lax.broadcasted_iota(jnp.int32, (1, LANES), 1) < half
    cos_ref[0:r, :] = jnp.where(low, c, c_sw)
    cos_ref[r:2 * r, :] = jnp.where(low, c_sw, c)
    sin_ref[0:r, :] = jnp.where(low, -s, s_sw)
    sin_ref[r:2 * r, :] = jnp.where(low, -s_sw, s)


def _rope_tables(positions):
    t = positions.size
    half = RET_HEAD_DIM // 2
    inv_freq = ROPE_BASE ** (-jnp.arange(0, RET_HEAD_DIM, 2, dtype=_F32) / RET_HEAD_DIM)
    rows = math.gcd(t // 2, 512)
    pos2 = positions.reshape(t // (2 * rows), 2, rows).transpose(0, 2, 1).reshape(t // 2, 2)
    pos2 = jnp.repeat(pos2.astype(_F32), half, axis=1)
    inv2 = jnp.tile(inv_freq, 2).reshape(1, LANES)
    return pl.pallas_call(
        _rope_kernel,
        out_shape=[jax.ShapeDtypeStruct((t, LANES), _F32)] * 2,
        grid=(t // (2 * rows),),
        in_specs=[pl.BlockSpec((rows, LANES), lambda i: (i, 0)),
                  pl.BlockSpec((1, LANES), lambda i: (0, 0))],
        out_specs=[pl.BlockSpec((2 * rows, LANES), lambda i: (i, 0))] * 2,
        name="rope_tables",
    )(pos2, inv2)


CAST_COLS = 512


def _cast_weights(w_refs, w_scr):
    @pl.when(pl.program_id(0) == 0)
    def _():
        base = 0
        for w_ref in w_refs:
            width = w_ref.shape[-1]
            for c0 in range(0, width, CAST_COLS):
                c1 = min(c0 + CAST_COLS, width)
                block = w_ref[0, :, c0:c1] if len(w_ref.shape) == 3 else w_ref[:, c0:c1]
                w_scr[:, base + c0:base + c1] = block.astype(w_scr.dtype)
            base += width


def _inproj_kernel(*refs, epilogues, has_rope, n_flog, n_weights):
    x_ref, g_ref = refs[:2]
    w_refs = refs[2:2 + n_weights]
    rest = list(refs[2 + n_weights:])
    cos_ref, sin_ref = (rest.pop(0), rest.pop(0)) if has_rope else (None, None)
    o_ref = rest.pop(0)
    flog_ref = rest.pop(0) if n_flog else None
    w_scr = rest.pop(0)
    _cast_weights(w_refs, w_scr)
    nb = len(epilogues)
    sub = ROW_TILE // ROW_GROUPS
    groups = [pl.ds(part * sub, sub) for part in range(ROW_GROUPS)]

    def pre_norm(rows):
        x = x_ref[rows, :]
        return (x * _rms_scale(x) * g_ref[...]).astype(_BF16)

    def finish(rows, j, a):
        kind, scale = epilogues[j]
        if kind == "rot":
            a = a * cos_ref[rows, :] + pltpu.roll(a, RET_HEAD_DIM // 2, 1) * sin_ref[rows, :]
        if scale != 1.0:
            a = a * scale
        o_ref[j, rows, :] = a.astype(o_ref.dtype)

    hs = [pre_norm(rows) for rows in groups]
    for rows, h in zip(groups, hs):
        for jb in range(nb // 2):
            acc = jnp.dot(h, w_scr[:, jb * MXU_COLS:(jb + 1) * MXU_COLS], preferred_element_type=_F32)
            finish(rows, 2 * jb, acc[:, :LANES])
            finish(rows, 2 * jb + 1, acc[:, LANES:])
        if n_flog:
            base = nb * LANES
            flog_ref[rows, :] = jnp.dot(h, w_scr[:, base:base + LANES], preferred_element_type=_F32)


def _inproj(x2, g, weights, epilogues, rope=None, n_flog=0):
    t = x2.shape[0]
    nb = len(epilogues)
    ncols = sum(spec.block_shape[-1] for _, spec in weights)
    assert ncols == (nb + n_flog) * LANES
    in_specs = [pl.BlockSpec((ROW_TILE, D_MODEL), lambda i: (i, 0)), _const_spec((1, D_MODEL))]
    in_specs += [spec for _, spec in weights]
    args = [x2, g.reshape(1, D_MODEL)] + [w for w, _ in weights]
    if rope is not None:
        in_specs += [pl.BlockSpec((ROW_TILE, LANES), lambda i: (i, 0))] * 2
        args += list(rope)
    out_shape = [jax.ShapeDtypeStruct((nb, t, LANES), _BF16)]
    out_specs = [pl.BlockSpec((nb, ROW_TILE, LANES), lambda i: (0, i, 0))]
    if n_flog:
        out_shape.append(jax.ShapeDtypeStruct((t, LANES), _F32))
        out_specs.append(pl.BlockSpec((ROW_TILE, LANES), lambda i: (i, 0)))
    return pl.pallas_call(
        functools.partial(_inproj_kernel, epilogues=tuple(epilogues), has_rope=rope is not None, n_flog=n_flog,
                          n_weights=len(weights)),
        out_shape=out_shape,
        grid=(t // ROW_TILE,),
        in_specs=in_specs,
        out_specs=out_specs,
        scratch_shapes=[pltpu.VMEM((D_MODEL, ncols), _BF16)],
        compiler_params=pltpu.CompilerParams(dimension_semantics=("arbitrary",), vmem_limit_bytes=VMEM_LIMIT_BYTES),
        name="inproj_rope" if rope is not None else "inproj",
    )(*args)


def _layer_spec(shape, layer, width=None):
    rows, cols = shape[1], shape[2] if width is None else width
    return pl.BlockSpec((1, rows, cols), lambda *_: (layer, 0, 0), pipeline_mode=pl.Buffered(1))


def _retention_kernel(q_ref, k_ref, v_ref, gate_ref, dmask_ref, qd_ref, kd_ref, cd_ref, o_ref, *, n_chunks):
    c = RET_CHUNK
    dmask = dmask_ref[0]
    qd = qd_ref[0]
    kd = kd_ref[0]
    cd = cd_ref[0]

    inner, kv = [], []
    for n in range(n_chunks):
        rows = pl.ds(n * c, c)
        qc, kc, vc = q_ref[0, rows, :], k_ref[0, rows, :], v_ref[0, rows, :]
        inner.append((lax.dot_general(qc, kc, _NT, preferred_element_type=_F32) * dmask).astype(_BF16))
        k_scaled_t = (kc.astype(_F32) * kd).T.astype(_BF16)
        kv.append(jnp.dot(k_scaled_t, vc, preferred_element_type=_F32))

    state = jnp.zeros((RET_HEAD_DIM, RET_HEAD_DIM), _F32)
    for n in range(n_chunks):
        rows = pl.ds(n * c, c)
        q_scaled = (q_ref[0, rows, :].astype(_F32) * qd).astype(_BF16)
        lhs = jnp.concatenate([inner[n], q_scaled], axis=1)
        rhs = jnp.concatenate([v_ref[0, rows, :], state.astype(_BF16)], axis=0)
        out = jnp.dot(lhs, rhs, preferred_element_type=_F32)
        state = state * cd + kv[n]
        r = out * _rms_scale(out)
        g = gate_ref[0, rows, :].astype(_F32)
        o_ref[0, rows, :] = (r * (g * jax.nn.sigmoid(g))).astype(o_ref.dtype)


def _retention_tables():
    c = RET_CHUNK
    log_gamma = jnp.log(1.0 - 2.0 ** (-5.0 - jnp.arange(RET_HEADS, dtype=_F32)))
    idx = jnp.arange(c, dtype=_F32)
    rel = idx[:, None] - idx[None, :]
    dmask = jnp.where(rel[None] >= 0, jnp.exp(jnp.maximum(rel, 0.0)[None] * log_gamma[:, None, None]), 0.0)
    q_decay = jnp.exp((idx[None, :] + 1.0) * log_gamma[:, None])
    k_decay = jnp.exp((c - 1.0 - idx[None, :]) * log_gamma[:, None])
    chunk_decay = jnp.exp(c * log_gamma)
    full = (RET_HEADS, c, RET_HEAD_DIM)
    return (dmask,
            jnp.broadcast_to(q_decay[:, :, None], full),
            jnp.broadcast_to(k_decay[:, :, None], full),
            jnp.broadcast_to(chunk_decay[:, None, None], full))


def _retention(proj, batch, seq):
    nh = RET_HEADS
    p4 = proj.reshape(proj.shape[0], batch, seq, LANES)

    def slab(offset):
        return pl.BlockSpec((1, 1, seq, LANES), lambda b, h: (offset + h, b, 0, 0))

    def table():
        return pl.BlockSpec((1, RET_CHUNK, RET_HEAD_DIM), lambda b, h: (h, 0, 0))

    def body(q_ref, k_ref, v_ref, gate_ref, dm, qd, kd, cd, o_ref):
        _retention_kernel(q_ref.at[0], k_ref.at[0], v_ref.at[0], gate_ref.at[0], dm, qd, kd, cd, o_ref.at[0],
                          n_chunks=seq // RET_CHUNK)

    out = pl.pallas_call(
        body,
        out_shape=jax.ShapeDtypeStruct((nh, batch, seq, LANES), _BF16),
        grid=(batch, nh),
        in_specs=[slab(0), slab(nh), slab(2 * nh), slab(3 * nh), table(), table(), table(), table()],
        out_specs=pl.BlockSpec((1, 1, seq, LANES), lambda b, h: (h, b, 0, 0)),
        compiler_params=pltpu.CompilerParams(dimension_semantics=("arbitrary", "arbitrary"),
                                             vmem_limit_bytes=VMEM_LIMIT_BYTES),
        name="retention",
    )(p4, p4, p4, p4, *_retention_tables())
    return out.reshape(nh, batch * seq, LANES)


def _split3(x):
    hi = x.astype(_BF16)
    r1 = x - hi.astype(_F32)
    mid = r1.astype(_BF16)
    lo = (r1 - mid.astype(_F32)).astype(_BF16)
    return hi, mid, lo


C_PART_STRIDE = 16
ONES_LANE = LANES - 1


def _fox_kernel(q_ref, k_ref, v_ref, flog_ref, bias_ref, sel_ref, tri_ref, o_ref,
                cpack_ref, qa_ref, ka_ref, va_ref, *, seq):
    pair = pl.program_id(1)
    lane = lax.broadcasted_iota(jnp.int32, (1, LANES), 1)
    low_half = lane < FOX_HEAD_DIM
    keep = (low_half, jnp.logical_not(low_half))
    sum_lane = (FOX_HEAD_DIM, 0)

    @pl.when(pair == 0)
    def _():
        tri = tri_ref[...]
        head_lanes = lane < C_PART_STRIDE
        carry = jnp.zeros((1, LANES), _F32)
        for i in range(seq // CUM_TILE):
            rows = pl.ds(i * CUM_TILE, CUM_TILE)
            z = flog_ref[0, rows, :] + bias_ref[...]
            log_f = jnp.minimum(z, 0.0) - jnp.log1p(jnp.exp(-jnp.abs(z)))
            hi, mid, lo = _split3(log_f)
            cs = (jnp.dot(tri, hi, preferred_element_type=_F32)
                  + jnp.dot(tri, mid, preferred_element_type=_F32)
                  + jnp.dot(tri, lo, preferred_element_type=_F32)) + carry
            carry = cs[CUM_TILE - 1:CUM_TILE, :]
            parts = [jnp.where(head_lanes, part.astype(_F32), 0.0) for part in _split3(cs * LOG2E)]
            packed = (parts[0] + pltpu.roll(parts[1], C_PART_STRIDE, 1) + pltpu.roll(parts[2], 2 * C_PART_STRIDE, 1))
            cpack_ref[rows, :] = jnp.where(lane == ONES_LANE, 1.0, packed).astype(_BF16)

    blk = 512
    for r in range(seq // blk):
        rows = pl.ds(r * blk, blk)
        extras = jnp.dot(cpack_ref[rows, :], sel_ref[0], preferred_element_type=_F32).astype(_BF16)
        for e in range(2):
            qa_ref[e, rows, :] = jnp.where(keep[e], q_ref[0, rows, :], extras[:, (2 * e) * LANES:(2 * e + 1) * LANES])
            ka_ref[e, rows, :] = jnp.where(keep[e], k_ref[0, rows, :], extras[:, (2 * e + 1) * LANES:(2 * e + 2) * LANES])
            ones_col = jnp.where(lane == sum_lane[e], 1.0, 0.0).astype(_BF16)
            va_ref[e, rows, :] = jnp.where(keep[e], v_ref[0, rows, :], ones_col)

    t = FOX_TILE
    row_id = lax.broadcasted_iota(jnp.int32, (t, t), 0)
    col_id = lax.broadcasted_iota(jnp.int32, (t, t), 1)
    causal = col_id <= row_id

    for qi in range(seq // t):
        q_rows = pl.ds(qi * t, t)
        acc = []
        for e in range(2):
            qa = qa_ref[e, q_rows, :]
            m_run = jnp.full((t, 1), NEG_BIG, _F32)
            a = jnp.zeros((t, LANES), _F32)
            for kj in range(qi + 1):
                k_rows = pl.ds(kj * t, t)
                s = lax.dot_general(qa, ka_ref[e, k_rows, :], _NT, preferred_element_type=_F32)
                if kj == qi:
                    s = jnp.where(causal, s, NEG_BIG)
                m_new = jnp.maximum(m_run, jnp.max(s, axis=-1, keepdims=True))
                p = jnp.exp2(s - m_new).astype(_BF16)
                a = a * jnp.exp2(m_run - m_new) + jnp.dot(p, va_ref[e, k_rows, :], preferred_element_type=_F32)
                m_run = m_new
            acc.append(a / a[:, sum_lane[e]:sum_lane[e] + 1])
        o_ref[0, q_rows, :] = jnp.where(low_half, acc[0], acc[1]).astype(o_ref.dtype)


def _fox_selectors():
    import numpy as np
    npair = FOX_HEADS // 2
    sel = np.zeros((npair, LANES, 4 * LANES), np.float32)
    for h in range(FOX_HEADS):
        p, e = divmod(h, 2)
        base = FOX_HEAD_DIM if e == 0 else 0
        qcol, kcol = (2 * e) * LANES + base, (2 * e + 1) * LANES + base
        for part in range(3):
            sel[p, part * C_PART_STRIDE + h, qcol + part] = 1.0
            sel[p, ONES_LANE, kcol + part] = 1.0
            sel[p, ONES_LANE, qcol + 3 + part] = 1.0
            sel[p, part * C_PART_STRIDE + h, kcol + 3 + part] = -1.0
    return jnp.asarray(sel, _BF16)


def _fox(proj, flog, b_forget, batch, seq):
    npair = FOX_HEADS // 2
    p4 = proj.reshape(proj.shape[0], batch, seq, LANES)
    flog3 = flog.reshape(batch, seq, LANES)
    bias = jnp.zeros((1, LANES), _F32).at[0, :FOX_HEADS].set(b_forget.astype(_F32))
    tri = jnp.tril(jnp.ones((CUM_TILE, CUM_TILE), _F32)).astype(_BF16)

    def slab(offset):
        return pl.BlockSpec((1, 1, seq, LANES), lambda b, p: (offset + p, b, 0, 0))

    def body(q_ref, k_ref, v_ref, flog_ref, bias_ref, sel_ref, tri_ref, o_ref, *scratch):
        _fox_kernel(q_ref.at[0], k_ref.at[0], v_ref.at[0], flog_ref, bias_ref, sel_ref, tri_ref,
                    o_ref.at[0], *scratch, seq=seq)

    out = pl.pallas_call(
        body,
        out_shape=jax.ShapeDtypeStruct((npair, batch, seq, LANES), _BF16),
        grid=(batch, npair),
        in_specs=[slab(0), slab(npair), slab(2 * npair),
                  pl.BlockSpec((1, seq, LANES), lambda b, p: (b, 0, 0)),
                  _const_spec((1, LANES)),
                  pl.BlockSpec((1, LANES, 4 * LANES), lambda b, p: (p, 0, 0)),
                  _const_spec((CUM_TILE, CUM_TILE))],
        out_specs=pl.BlockSpec((1, 1, seq, LANES), lambda b, p: (p, b, 0, 0)),
        scratch_shapes=[pltpu.VMEM((seq, LANES), _BF16),
                        pltpu.VMEM((2, seq, LANES), _BF16),
                        pltpu.VMEM((2, seq, LANES), _BF16),
                        pltpu.VMEM((2, seq, LANES), _BF16)],
        compiler_params=pltpu.CompilerParams(dimension_semantics=("arbitrary", "arbitrary"),
                                             vmem_limit_bytes=VMEM_LIMIT_BYTES),
        name="fox_attention",
    )(p4, p4, p4, flog3, bias, _fox_selectors(), tri)
    return out.reshape(npair, batch * seq, LANES)


def _memattn_kernel(mem_ref, g_ref, w_ref, mq_ref, o_ref, w_scr, *, seq):
    _cast_weights([w_ref], w_scr)
    mem = mem_ref[0]
    kvn = (mem * _rms_scale(mem) * g_ref[...]).astype(_BF16)
    kv = jnp.dot(kvn, w_scr[...], preferred_element_type=_F32)
    lane = lax.broadcasted_iota(jnp.int32, (1, LANES), 1)
    low_half = lane < MEM_HEAD_DIM
    scale = MEM_HEAD_DIM ** -0.5
    for p in range(N_MEM_BLOCKS):
        mk = (kv[:, p * LANES:(p + 1) * LANES] * scale).astype(_BF16)
        mv = kv[:, MEM_WIDTH + p * LANES:MEM_WIDTH + (p + 1) * LANES].astype(_BF16)
        heads = []
        for e in range(2):
            keep = low_half if e == 0 else jnp.logical_not(low_half)
            sum_lane = MEM_HEAD_DIM if e == 0 else 0
            ones_col = jnp.where(lane == sum_lane, 1.0, 0.0).astype(_BF16)
            heads.append((jnp.where(keep, mk, jnp.zeros_like(mk)), jnp.where(keep, mv, ones_col), sum_lane))
        row_tiles = [pl.ds(i * MEM_ROWS, MEM_ROWS) for i in range(seq // MEM_ROWS)]
        scores = [[lax.dot_general(mq_ref[p, rows, :], mk_e, _NT, preferred_element_type=_F32)
                   for mk_e, _, _ in heads] for rows in row_tiles]
        probs = [[jnp.exp(s - jnp.max(s, axis=-1, keepdims=True)).astype(_BF16) for s in pair] for pair in scores]
        for rows, pair in zip(row_tiles, probs):
            outs = []
            for prob, (_, mv_e, sum_lane) in zip(pair, heads):
                out = jnp.dot(prob, mv_e, preferred_element_type=_F32)
                outs.append(out / out[:, sum_lane:sum_lane + 1])
            o_ref[p, rows, :] = jnp.where(low_half, outs[0], outs[1]).astype(o_ref.dtype)


def _memattn(proj, mq_offset, mem, g_mem, w_mem_kv, layer, batch, seq):
    p4 = proj.reshape(proj.shape[0], batch, seq, LANES)
    mem_len = mem.shape[1]

    def body(mem_ref, g_ref, w_ref, mq_ref, o_ref, w_scr):
        _memattn_kernel(mem_ref, g_ref, w_ref, mq_ref.at[:, 0], o_ref.at[:, 0], w_scr, seq=seq)

    out = pl.pallas_call(
        body,
        out_shape=jax.ShapeDtypeStruct((N_MEM_BLOCKS, batch, seq, LANES), _BF16),
        grid=(batch,),
        in_specs=[pl.BlockSpec((1, mem_len, D_MODEL), lambda b: (b, 0, 0)),
                  _const_spec((1, D_MODEL)),
                  _layer_spec(w_mem_kv.shape, layer),
                  pl.BlockSpec((N_MEM_BLOCKS, 1, seq, LANES), lambda b: (mq_offset // N_MEM_BLOCKS, b, 0, 0))],
        out_specs=pl.BlockSpec((N_MEM_BLOCKS, 1, seq, LANES), lambda b: (0, b, 0, 0)),
        scratch_shapes=[pltpu.VMEM((D_MODEL, 2 * MEM_WIDTH), _BF16)],
        compiler_params=pltpu.CompilerParams(dimension_semantics=("arbitrary",), vmem_limit_bytes=VMEM_LIMIT_BYTES),
        name="memory_attention",
    )(mem, g_mem.reshape(1, D_MODEL), w_mem_kv, p4)
    return out.reshape(N_MEM_BLOCKS, batch * seq, LANES)


def _tail_kernel(x_ref, so_ref, mo_ref, wo_ref, g1_ref, g2_ref, g3_ref, wup_ref, wdn_ref, o_ref):
    sub = ROW_TILE // ROW_GROUPS
    groups = [pl.ds(part * sub, sub) for part in range(ROW_GROUPS)]

    def out_proj(rows):
        mixed = jnp.concatenate([so_ref[j, rows, :] for j in range(N_SELF_BLOCKS)]
                                + [mo_ref[j, rows, :] for j in range(N_MEM_BLOCKS)], axis=1)
        return jnp.dot(mixed, wo_ref[0], preferred_element_type=_F32)

    def mid_norms(rows, y):
        x1 = x_ref[rows, :] + y * _rms_scale(y) * g1_ref[...]
        return x1, (x1 * _rms_scale(x1) * g2_ref[...]).astype(_BF16)

    def mlp(h):
        acc = jnp.zeros((sub, D_MODEL), _F32)
        for c in range(D_FF // FF_CHUNK):
            cols = slice(c * FF_CHUNK, (c + 1) * FF_CHUNK)
            u = jnp.maximum(jnp.dot(h, wup_ref[0, :, cols], preferred_element_type=_F32), 0.0)
            acc = acc + jnp.dot((u * u).astype(_BF16), wdn_ref[0, cols, :], preferred_element_type=_F32)
        return acc

    ys = [out_proj(rows) for rows in groups]
    mids = [mid_norms(rows, y) for rows, y in zip(groups, ys)]
    for rows, (x1, h) in zip(groups, mids):
        acc = mlp(h)
        o_ref[rows, :] = x1 + acc * _rms_scale(acc) * g3_ref[...]


def _tail(x2, self_out, mem_out, layer, w_out, g_post_mix, g_pre_mlp, g_post_mlp, w_up, w_down):
    t = x2.shape[0]
    row = lambda: pl.BlockSpec((ROW_TILE, D_MODEL), lambda i: (i, 0))
    gain = lambda g: g.reshape(1, D_MODEL)
    return pl.pallas_call(
        _tail_kernel,
        out_shape=jax.ShapeDtypeStruct((t, D_MODEL), _F32),
        grid=(t // ROW_TILE,),
        in_specs=[row(),
                  pl.BlockSpec((N_SELF_BLOCKS, ROW_TILE, LANES), lambda i: (0, i, 0)),
                  pl.BlockSpec((N_MEM_BLOCKS, ROW_TILE, LANES), lambda i: (0, i, 0)),
                  _layer_spec(w_out.shape, layer),
                  _const_spec((1, D_MODEL)), _const_spec((1, D_MODEL)), _const_spec((1, D_MODEL)),
                  _layer_spec(w_up.shape, layer), _layer_spec(w_down.shape, layer)],
        out_specs=row(),
        compiler_params=pltpu.CompilerParams(dimension_semantics=("arbitrary",), vmem_limit_bytes=VMEM_LIMIT_BYTES),
        name="outproj_mlp",
    )(x2, self_out, mem_out, w_out, gain(g_post_mix), gain(g_pre_mlp), gain(g_post_mlp), w_up, w_down)


def kernel(x, mem, positions, w_in_ret, w_in_fox, b_forget, w_mem_kv, w_out, w_up, w_down,
           g_pre_mix, g_post_mix, g_pre_mlp, g_post_mlp, g_mem):
    batch, seq, _ = x.shape
    depth = w_out.shape[0]
    assert (batch * seq) % ROW_TILE == 0 and seq % FOX_TILE == 0 and seq % RET_CHUNK == 0
    x2 = x.reshape(batch * seq, D_MODEL)
    rope = _rope_tables(positions)
    w_out_b, w_up_b, w_down_b = w_out.astype(_BF16), w_up.astype(_BF16), w_down.astype(_BF16)
    for i in range(depth):
        j = i // 2
        if i % 2 == 0:
            epi = ([("rot", 1.0)] * RET_HEADS + [("rot", RET_HEAD_DIM ** -0.5)] * RET_HEADS
                   + [("plain", 1.0)] * (2 * RET_HEADS + N_MEM_BLOCKS))
            (proj,) = _inproj(x2, g_pre_mix[i], [(w_in_ret, _layer_spec(w_in_ret.shape, j))], epi, rope=rope)
            self_out = _retention(proj, batch, seq)
            mq_offset = 4 * RET_HEADS
        else:
            qkv = 3 * FOX_WIDTH
            w = w_in_fox[j]
            w_rest = jnp.concatenate([w[:, qkv + FOX_HEADS:],
                                      jnp.pad(w[:, qkv:qkv + FOX_HEADS], ((0, 0), (0, LANES - FOX_HEADS)))], axis=1)
            npair = FOX_HEADS // 2
            epi = ([("plain", FOX_HEAD_DIM ** -0.5 * LOG2E)] * npair + [("plain", 1.0)] * (2 * npair + N_MEM_BLOCKS))
            weights = [(w_in_fox, _layer_spec(w_in_fox.shape, j, width=qkv)), (w_rest, _const_spec(w_rest.shape))]
            proj, flog = _inproj(x2, g_pre_mix[i], weights, epi, n_flog=1)
            self_out = _fox(proj, flog, b_forget[j], batch, seq)
            mq_offset = 3 * npair
        mem_out = _memattn(proj, mq_offset, mem, g_mem[i], w_mem_kv, i, batch, seq)
        x2 = _tail(x2, self_out, mem_out, i, w_out_b, g_post_mix[i], g_pre_mlp[i], g_post_mlp[i], w_up_b, w_down_b)
    return x2.reshape(batch, seq, D_MODEL)
```

```python
import functools
import math

import jax
import jax.numpy as jnp
from jax import lax
from jax.experimental import pallas as pl
from jax.experimental.pallas import tpu as pltpu

D_MODEL = 1024
MEM_HEADS = 4
MEM_HEAD_DIM = 64
MEM_WIDTH = MEM_HEADS * MEM_HEAD_DIM
RET_HEAD_DIM = 128
RET_HEADS = 6
RET_WIDTH = RET_HEADS * RET_HEAD_DIM
RET_CHUNK = 128
ROPE_BASE = 10000.0
FOX_HEAD_DIM = 64
FOX_HEADS = 12
FOX_WIDTH = FOX_HEADS * FOX_HEAD_DIM
D_FF = 4 * D_MODEL
EPS = 1e-6

LANES = 128
MXU_COLS = 256
VMEM_LIMIT_BYTES = 56 * 1024 * 1024
LOG2E = 1.4426950408889634
NEG_BIG = -1e30

ROW_TILE = 1024
ROW_GROUPS = 2
FF_CHUNK = 512
FOX_TILE = 256
CUM_TILE = 256
MEM_ROWS = 256

N_SELF_BLOCKS = RET_WIDTH // LANES
N_MEM_BLOCKS = MEM_WIDTH // LANES

_NT = (((1,), (1,)), ((), ()))
_F32 = jnp.float32
_BF16 = jnp.bfloat16


def _rms_scale(x):
    return lax.rsqrt(jnp.mean(x * x, axis=-1, keepdims=True) + EPS)


def _const_spec(shape):
    nd = len(shape)
    return pl.BlockSpec(shape, lambda *_: (0,) * nd, pipeline_mode=pl.Buffered(1))


def _rope_kernel(pos_ref, inv_ref, cos_ref, sin_ref):
    r = pos_ref.shape[0]
    ang = pos_ref[...] * inv_ref[...]
    c, s = jnp.cos(ang), jnp.sin(ang)
    half = RET_HEAD_DIM // 2
    c_sw, s_sw = pltpu.roll(c, half, 1), pltpu.roll(s, half, 1)
    low = lax.broadcasted_iota(jnp.int32, (1, LANES), 1) < half
    cos_ref[0:r, :] = jnp.where(low, c, c_sw)
    cos_ref[r:2 * r, :] = jnp.where(low, c_sw, c)
    sin_ref[0:r, :] = jnp.where(low, -s, s_sw)
    sin_ref[r:2 * r, :] = jnp.where(low, -s_sw, s)


def _rope_tables(positions):
    t = positions.size
    half = RET_HEAD_DIM // 2
    inv_freq = ROPE_BASE ** (-jnp.arange(0, RET_HEAD_DIM, 2, dtype=_F32) / RET_HEAD_DIM)
    rows = math.gcd(t // 2, 512)
    pos2 = positions.reshape(t // (2 * rows), 2, rows).transpose(0, 2, 1).reshape(t // 2, 2)
    pos2 = jnp.repeat(pos2.astype(_F32), half, axis=1)
    inv2 = jnp.tile(inv_freq, 2).reshape(1, LANES)
    return pl.pallas_call(
        _rope_kernel,
        out_shape=[jax.ShapeDtypeStruct((t, LANES), _F32)] * 2,
        grid=(t // (2 * rows),),
        in_specs=[pl.BlockSpec((rows, LANES), lambda i: (i, 0)),
                  pl.BlockSpec((1, LANES), lambda i: (0, 0))],
        out_specs=[pl.BlockSpec((2 * rows, LANES), lambda i: (i, 0))] * 2,
        name="rope_tables",
    )(pos2, inv2)


CAST_COLS = 512


def _cast_weights(w_refs, w_scr, transposed=()):
    @pl.when(pl.program_id(0) == 0)
    def _():
        base = 0
        for n, w_ref in enumerate(w_refs):
            w2 = w_ref.at[0] if len(w_ref.shape) == 3 else w_ref
            width = w2.shape[0] if n in transposed else w2.shape[1]
            for c0 in range(0, width, CAST_COLS):
                c1 = min(c0 + CAST_COLS, width)
                block = w2[c0:c1, :].T if n in transposed else w2[:, c0:c1]
                w_scr[:, base + c0:base + c1] = block.astype(w_scr.dtype)
            base += width


def _inproj_kernel(*refs, epilogues, has_rope, n_flog, n_weights, transposed):
    x_ref, g_ref = refs[:2]
    w_refs = refs[2:2 + n_weights]
    rest = list(refs[2 + n_weights:])
    cos_ref, sin_ref = (rest.pop(0), rest.pop(0)) if has_rope else (None, None)
    o_ref = rest.pop(0)
    flog_ref = rest.pop(0) if n_flog else None
    w_scr = rest.pop(0)
    _cast_weights(w_refs, w_scr, transposed)
    nb = len(epilogues)
    sub = ROW_TILE // ROW_GROUPS
    groups = [pl.ds(part * sub, sub) for part in range(ROW_GROUPS)]

    def pre_norm(rows):
        x = x_ref[rows, :]
        return (x * _rms_scale(x) * g_ref[...]).astype(_BF16)

    def finish(rows, j, a):
        kind, scale = epilogues[j]
        if kind == "rot":
            a = a * cos_ref[rows, :] + pltpu.roll(a, RET_HEAD_DIM // 2, 1) * sin_ref[rows, :]
        if scale != 1.0:
            a = a * scale
        o_ref[j, rows, :] = a.astype(o_ref.dtype)

    hs = [pre_norm(rows) for rows in groups]
    for rows, h in zip(groups, hs):
        for jb in range(nb // 2):
            acc = jnp.dot(h, w_scr[:, jb * MXU_COLS:(jb + 1) * MXU_COLS], preferred_element_type=_F32)
            finish(rows, 2 * jb, acc[:, :LANES])
            finish(rows, 2 * jb + 1, acc[:, LANES:])
        if n_flog:
            base = nb * LANES
            flog_ref[rows, :] = jnp.dot(h, w_scr[:, base:base + LANES], preferred_element_type=_F32)


def _inproj(x2, g, weights, epilogues, rope=None, n_flog=0, transposed=()):
    t = x2.shape[0]
    nb = len(epilogues)
    ncols = sum(spec.block_shape[-2 if n in transposed else -1] for n, (_, spec) in enumerate(weights))
    assert ncols == (nb + n_flog) * LANES
    in_specs = [pl.BlockSpec((ROW_TILE, D_MODEL), lambda i: (i, 0)), _const_spec((1, D_MODEL))]
    in_specs += [spec for _, spec in weights]
    args = [x2, g.reshape(1, D_MODEL)] + [w for w, _ in weights]
    if rope is not None:
        in_specs += [pl.BlockSpec((ROW_TILE, LANES), lambda i: (i, 0))] * 2
        args += list(rope)
    out_shape = [jax.ShapeDtypeStruct((nb, t, LANES), _BF16)]
    out_specs = [pl.BlockSpec((nb, ROW_TILE, LANES), lambda i: (0, i, 0))]
    if n_flog:
        out_shape.append(jax.ShapeDtypeStruct((t, LANES), _F32))
        out_specs.append(pl.BlockSpec((ROW_TILE, LANES), lambda i: (i, 0)))
    return pl.pallas_call(
        functools.partial(_inproj_kernel, epilogues=tuple(epilogues), has_rope=rope is not None, n_flog=n_flog,
                          n_weights=len(weights), transposed=tuple(transposed)),
        out_shape=out_shape,
        grid=(t // ROW_TILE,),
        in_specs=in_specs,
        out_specs=out_specs,
        scratch_shapes=[pltpu.VMEM((D_MODEL, ncols), _BF16)],
        compiler_params=pltpu.CompilerParams(dimension_semantics=("arbitrary",), vmem_limit_bytes=VMEM_LIMIT_BYTES),
        name="inproj_rope" if rope is not None else "inproj",
    )(*args)


def _layer_spec(shape, layer, rows=None):
    return pl.BlockSpec((1, shape[1] if rows is None else rows, shape[2]), lambda *_: (layer, 0, 0),
                        pipeline_mode=pl.Buffered(1))


def _retention_kernel(q_ref, k_ref, v_ref, gate_ref, dmask_ref, qd_ref, kd_ref, cd_ref, o_ref, *, n_chunks):
    c = RET_CHUNK
    dmask = dmask_ref[0]
    qd = qd_ref[0]
    kd = kd_ref[0]
    cd = cd_ref[0]

    inner, kv = [], []
    for n in range(n_chunks):
        rows = pl.ds(n * c, c)
        qc, kc, vc = q_ref[0, rows, :], k_ref[0, rows, :], v_ref[0, rows, :]
        inner.append((lax.dot_general(qc, kc, _NT, preferred_element_type=_F32) * dmask).astype(_BF16))
        k_scaled_t = (kc.astype(_F32) * kd).T.astype(_BF16)
        kv.append(jnp.dot(k_scaled_t, vc, preferred_element_type=_F32))

    state = jnp.zeros((RET_HEAD_DIM, RET_HEAD_DIM), _F32)
    for n in range(n_chunks):
        rows = pl.ds(n * c, c)
        q_scaled = (q_ref[0, rows, :].astype(_F32) * qd).astype(_BF16)
        lhs = jnp.concatenate([inner[n], q_scaled], axis=1)
        rhs = jnp.concatenate([v_ref[0, rows, :], state.astype(_BF16)], axis=0)
        out = jnp.dot(lhs, rhs, preferred_element_type=_F32)
        state = state * cd + kv[n]
        r = out * _rms_scale(out)
        g = gate_ref[0, rows, :].astype(_F32)
        o_ref[0, rows, :] = (r * (g * jax.nn.sigmoid(g))).astype(o_ref.dtype)


def _retention_tables():
    c = RET_CHUNK
    log_gamma = jnp.log(1.0 - 2.0 ** (-5.0 - jnp.arange(RET_HEADS, dtype=_F32)))
    idx = jnp.arange(c, dtype=_F32)
    rel = idx[:, None] - idx[None, :]
    dmask = jnp.where(rel[None] >= 0, jnp.exp(jnp.maximum(rel, 0.0)[None] * log_gamma[:, None, None]), 0.0)
    q_decay = jnp.exp((idx[None, :] + 1.0) * log_gamma[:, None])
    k_decay = jnp.exp((c - 1.0 - idx[None, :]) * log_gamma[:, None])
    chunk_decay = jnp.exp(c * log_gamma)
    full = (RET_HEADS, c, RET_HEAD_DIM)
    return (dmask,
            jnp.broadcast_to(q_decay[:, :, None], full),
            jnp.broadcast_to(k_decay[:, :, None], full),
            jnp.broadcast_to(chunk_decay[:, None, None], full))


def _retention(proj, batch, seq):
    nh = RET_HEADS
    p4 = proj.reshape(proj.shape[0], batch, seq, LANES)

    def slab(group):
        return pl.BlockSpec((nh, 1, seq, LANES), lambda b: (group, b, 0, 0))

    def table():
        return _const_spec((nh, RET_CHUNK, RET_HEAD_DIM))

    def body(q_ref, k_ref, v_ref, gate_ref, dm, qd, kd, cd, o_ref):
        for h in range(nh):
            one = pl.ds(h, 1)
            _retention_kernel(q_ref.at[h], k_ref.at[h], v_ref.at[h], gate_ref.at[h],
                              dm.at[one], qd.at[one], kd.at[one], cd.at[one], o_ref.at[h], n_chunks=seq // RET_CHUNK)

    out = pl.pallas_call(
        body,
        out_shape=jax.ShapeDtypeStruct((nh, batch, seq, LANES), _BF16),
        grid=(batch,),
        in_specs=[slab(0), slab(1), slab(2), slab(3), table(), table(), table(), table()],
        out_specs=pl.BlockSpec((nh, 1, seq, LANES), lambda b: (0, b, 0, 0)),
        compiler_params=pltpu.CompilerParams(dimension_semantics=("arbitrary",), vmem_limit_bytes=VMEM_LIMIT_BYTES),
        name="retention",
    )(p4, p4, p4, p4, *_retention_tables())
    return out.reshape(nh, batch * seq, LANES)


def _split3(x):
    hi = x.astype(_BF16)
    r1 = x - hi.astype(_F32)
    mid = r1.astype(_BF16)
    lo = (r1 - mid.astype(_F32)).astype(_BF16)
    return hi, mid, lo


C_PART_STRIDE = 16
ONES_LANE = LANES - 1


def _fox_kernel(q_ref, k_ref, v_ref, flog_ref, bias_ref, sel_ref, tri_ref, o_ref,
                cpack_ref, qa_ref, ka_ref, va_ref, *, seq):
    pair = pl.program_id(1)
    lane = lax.broadcasted_iota(jnp.int32, (1, LANES), 1)
    low_half = lane < FOX_HEAD_DIM
    keep = (low_half, jnp.logical_not(low_half))
    sum_lane = (FOX_HEAD_DIM, 0)

    @pl.when(pair == 0)
    def _():
        tri = tri_ref[...]
        head_lanes = lane < C_PART_STRIDE
        carry = jnp.zeros((1, LANES), _F32)
        for i in range(seq // CUM_TILE):
            rows = pl.ds(i * CUM_TILE, CUM_TILE)
            z = flog_ref[0, rows, :] + bias_ref[...]
            log_f = jnp.minimum(z, 0.0) - jnp.log1p(jnp.exp(-jnp.abs(z)))
            hi, mid, lo = _split3(log_f)
            cs = (jnp.dot(tri, hi, preferred_element_type=_F32)
                  + jnp.dot(tri, mid, preferred_element_type=_F32)
                  + jnp.dot(tri, lo, preferred_element_type=_F32)) + carry
            carry = cs[CUM_TILE - 1:CUM_TILE, :]
            parts = [jnp.where(head_lanes, part.astype(_F32), 0.0) for part in _split3(cs * LOG2E)]
            packed = (parts[0] + pltpu.roll(parts[1], C_PART_STRIDE, 1) + pltpu.roll(parts[2], 2 * C_PART_STRIDE, 1))
            cpack_ref[rows, :] = jnp.where(lane == ONES_LANE, 1.0, packed).astype(_BF16)

    blk = 512
    for r in range(seq // blk):
        rows = pl.ds(r * blk, blk)
        extras = jnp.dot(cpack_ref[rows, :], sel_ref[0], preferred_element_type=_F32).astype(_BF16)
        for e in range(2):
            qa_ref[e, rows, :] = jnp.where(keep[e], q_ref[0, rows, :], extras[:, (2 * e) * LANES:(2 * e + 1) * LANES])
            ka_ref[e, rows, :] = jnp.where(keep[e], k_ref[0, rows, :], extras[:, (2 * e + 1) * LANES:(2 * e + 2) * LANES])
            ones_col = jnp.where(lane == sum_lane[e], 1.0, 0.0).astype(_BF16)
            va_ref[e, rows, :] = jnp.where(keep[e], v_ref[0, rows, :], ones_col)

    t = FOX_TILE
    row_id = lax.broadcasted_iota(jnp.int32, (t, t), 0)
    col_id = lax.broadcasted_iota(jnp.int32, (t, t), 1)
    causal = col_id <= row_id

    for qi in range(seq // t):
        q_rows = pl.ds(qi * t, t)
        acc = []
        for e in range(2):
            qa = qa_ref[e, q_rows, :]
            m_run = jnp.full((t, 1), NEG_BIG, _F32)
            a = jnp.zeros((t, LANES), _F32)
            for kj in range(qi + 1):
                k_rows = pl.ds(kj * t, t)
                s = lax.dot_general(qa, ka_ref[e, k_rows, :], _NT, preferred_element_type=_F32)
                if kj == qi:
                    s = jnp.where(causal, s, NEG_BIG)
                m_new = jnp.maximum(m_run, jnp.max(s, axis=-1, keepdims=True))
                p = jnp.exp2(s - m_new).astype(_BF16)
                a = a * jnp.exp2(m_run - m_new) + jnp.dot(p, va_ref[e, k_rows, :], preferred_element_type=_F32)
                m_run = m_new
            acc.append(a / a[:, sum_lane[e]:sum_lane[e] + 1])
        o_ref[0, q_rows, :] = jnp.where(low_half, acc[0], acc[1]).astype(o_ref.dtype)


def _fox_selectors():
    import numpy as np
    npair = FOX_HEADS // 2
    sel = np.zeros((npair, LANES, 4 * LANES), np.float32)
    for h in range(FOX_HEADS):
        p, e = divmod(h, 2)
        base = FOX_HEAD_DIM if e == 0 else 0
        qcol, kcol = (2 * e) * LANES + base, (2 * e + 1) * LANES + base
        for part in range(3):
            sel[p, part * C_PART_STRIDE + h, qcol + part] = 1.0
            sel[p, ONES_LANE, kcol + part] = 1.0
            sel[p, ONES_LANE, qcol + 3 + part] = 1.0
            sel[p, part * C_PART_STRIDE + h, kcol + 3 + part] = -1.0
    return jnp.asarray(sel, _BF16)


def _fox(proj, flog, b_forget, batch, seq):
    npair = FOX_HEADS // 2
    p4 = proj.reshape(proj.shape[0], batch, seq, LANES)
    flog3 = flog.reshape(batch, seq, LANES)
    bias = jnp.zeros((1, LANES), _F32).at[0, :FOX_HEADS].set(b_forget.astype(_F32))
    tri = jnp.tril(jnp.ones((CUM_TILE, CUM_TILE), _F32)).astype(_BF16)

    def slab(offset):
        return pl.BlockSpec((1, 1, seq, LANES), lambda b, p: (offset + p, b, 0, 0))

    def body(q_ref, k_ref, v_ref, flog_ref, bias_ref, sel_ref, tri_ref, o_ref, *scratch):
        _fox_kernel(q_ref.at[0], k_ref.at[0], v_ref.at[0], flog_ref, bias_ref, sel_ref, tri_ref,
                    o_ref.at[0], *scratch, seq=seq)

    out = pl.pallas_call(
        body,
        out_shape=jax.ShapeDtypeStruct((npair, batch, seq, LANES), _BF16),
        grid=(batch, npair),
        in_specs=[slab(0), slab(npair), slab(2 * npair),
                  pl.BlockSpec((1, seq, LANES), lambda b, p: (b, 0, 0)),
                  _const_spec((1, LANES)),
                  pl.BlockSpec((1, LANES, 4 * LANES), lambda b, p: (p, 0, 0)),
                  _const_spec((CUM_TILE, CUM_TILE))],
        out_specs=pl.BlockSpec((1, 1, seq, LANES), lambda b, p: (p, b, 0, 0)),
        scratch_shapes=[pltpu.VMEM((seq, LANES), _BF16),
                        pltpu.VMEM((2, seq, LANES), _BF16),
                        pltpu.VMEM((2, seq, LANES), _BF16),
                        pltpu.VMEM((2, seq, LANES), _BF16)],
        compiler_params=pltpu.CompilerParams(dimension_semantics=("arbitrary", "arbitrary"),
                                             vmem_limit_bytes=VMEM_LIMIT_BYTES),
        name="fox_attention",
    )(p4, p4, p4, flog3, bias, _fox_selectors(), tri)
    return out.reshape(npair, batch * seq, LANES)


def _memattn_kernel(mem_ref, g_ref, w_ref, mq_ref, o_ref, w_scr, *, seq):
    _cast_weights([w_ref], w_scr)
    mem = mem_ref[0]
    kvn = (mem * _rms_scale(mem) * g_ref[...]).astype(_BF16)
    kv = jnp.dot(kvn, w_scr[...], preferred_element_type=_F32)
    lane = lax.broadcasted_iota(jnp.int32, (1, LANES), 1)
    low_half = lane < MEM_HEAD_DIM
    scale = MEM_HEAD_DIM ** -0.5
    for p in range(N_MEM_BLOCKS):
        mk = (kv[:, p * LANES:(p + 1) * LANES] * scale).astype(_BF16)
        mv = kv[:, MEM_WIDTH + p * LANES:MEM_WIDTH + (p + 1) * LANES].astype(_BF16)
        heads = []
        for e in range(2):
            keep = low_half if e == 0 else jnp.logical_not(low_half)
            sum_lane = MEM_HEAD_DIM if e == 0 else 0
            ones_col = jnp.where(lane == sum_lane, 1.0, 0.0).astype(_BF16)
            heads.append((jnp.where(keep, mk, jnp.zeros_like(mk)), jnp.where(keep, mv, ones_col), sum_lane))
        row_tiles = [pl.ds(i * MEM_ROWS, MEM_ROWS) for i in range(seq // MEM_ROWS)]
        scores = [[lax.dot_general(mq_ref[p, rows, :], mk_e, _NT, preferred_element_type=_F32)
                   for mk_e, _, _ in heads] for rows in row_tiles]
        probs = [[jnp.exp(s - jnp.max(s, axis=-1, keepdims=True)).astype(_BF16) for s in pair] for pair in scores]
        for rows, pair in zip(row_tiles, probs):
            outs = []
            for prob, (_, mv_e, sum_lane) in zip(pair, heads):
                out = jnp.dot(prob, mv_e, preferred_element_type=_F32)
                outs.append(out / out[:, sum_lane:sum_lane + 1])
            o_ref[p, rows, :] = jnp.where(low_half, outs[0], outs[1]).astype(o_ref.dtype)


def _memattn(proj, mq_offset, mem, g_mem, w_mem_kv, layer, batch, seq):
    p4 = proj.reshape(proj.shape[0], batch, seq, LANES)
    mem_len = mem.shape[1]

    def body(mem_ref, g_ref, w_ref, mq_ref, o_ref, w_scr):
        _memattn_kernel(mem_ref, g_ref, w_ref, mq_ref.at[:, 0], o_ref.at[:, 0], w_scr, seq=seq)

    out = pl.pallas_call(
        body,
        out_shape=jax.ShapeDtypeStruct((N_MEM_BLOCKS, batch, seq, LANES), _BF16),
        grid=(batch,),
        in_specs=[pl.BlockSpec((1, mem_len, D_MODEL), lambda b: (b, 0, 0)),
                  _const_spec((1, D_MODEL)),
                  _layer_spec(w_mem_kv.shape, layer),
                  pl.BlockSpec((N_MEM_BLOCKS, 1, seq, LANES), lambda b: (mq_offset // N_MEM_BLOCKS, b, 0, 0))],
        out_specs=pl.BlockSpec((N_MEM_BLOCKS, 1, seq, LANES), lambda b: (0, b, 0, 0)),
        scratch_shapes=[pltpu.VMEM((D_MODEL, 2 * MEM_WIDTH), _BF16)],
        compiler_params=pltpu.CompilerParams(dimension_semantics=("arbitrary",), vmem_limit_bytes=VMEM_LIMIT_BYTES),
        name="memory_attention",
    )(mem, g_mem.reshape(1, D_MODEL), w_mem_kv, p4)
    return out.reshape(N_MEM_BLOCKS, batch * seq, LANES)


def _tail_kernel(x_ref, so_ref, mo_ref, wo_ref, g1_ref, g2_ref, g3_ref, wup_ref, wdn_ref, o_ref):
    sub = ROW_TILE // ROW_GROUPS
    groups = [pl.ds(part * sub, sub) for part in range(ROW_GROUPS)]

    def out_proj(rows):
        mixed = jnp.concatenate([so_ref[j, rows, :] for j in range(N_SELF_BLOCKS)]
                                + [mo_ref[j, rows, :] for j in range(N_MEM_BLOCKS)], axis=1)
        return jnp.dot(mixed, wo_ref[0], preferred_element_type=_F32)

    def mid_norms(rows, y):
        x1 = x_ref[rows, :] + y * _rms_scale(y) * g1_ref[...]
        return x1, (x1 * _rms_scale(x1) * g2_ref[...]).astype(_BF16)

    def mlp(h):
        acc = jnp.zeros((sub, D_MODEL), _F32)
        for c in range(D_FF // FF_CHUNK):
            cols = slice(c * FF_CHUNK, (c + 1) * FF_CHUNK)
            u = jnp.maximum(jnp.dot(h, wup_ref[0, :, cols], preferred_element_type=_F32), 0.0)
            acc = acc + jnp.dot((u * u).astype(_BF16), wdn_ref[0, cols, :], preferred_element_type=_F32)
        return acc

    ys = [out_proj(rows) for rows in groups]
    mids = [mid_norms(rows, y) for rows, y in zip(groups, ys)]
    for rows, (x1, h) in zip(groups, mids):
        acc = mlp(h)
        o_ref[rows, :] = x1 + acc * _rms_scale(acc) * g3_ref[...]


def _tail(x2, self_out, mem_out, layer, w_out, g_post_mix, g_pre_mlp, g_post_mlp, w_up, w_down):
    t = x2.shape[0]
    row = lambda: pl.BlockSpec((ROW_TILE, D_MODEL), lambda i: (i, 0))
    gain = lambda g: g.reshape(1, D_MODEL)
    return pl.pallas_call(
        _tail_kernel,
        out_shape=jax.ShapeDtypeStruct((t, D_MODEL), _F32),
        grid=(t // ROW_TILE,),
        in_specs=[row(),
                  pl.BlockSpec((N_SELF_BLOCKS, ROW_TILE, LANES), lambda i: (0, i, 0)),
                  pl.BlockSpec((N_MEM_BLOCKS, ROW_TILE, LANES), lambda i: (0, i, 0)),
                  _layer_spec(w_out.shape, layer),
                  _const_spec((1, D_MODEL)), _const_spec((1, D_MODEL)), _const_spec((1, D_MODEL)),
                  _layer_spec(w_up.shape, layer), _layer_spec(w_down.shape, layer)],
        out_specs=row(),
        compiler_params=pltpu.CompilerParams(dimension_semantics=("arbitrary",), vmem_limit_bytes=VMEM_LIMIT_BYTES),
        name="outproj_mlp",
    )(x2, self_out, mem_out, w_out, gain(g_post_mix), gain(g_pre_mlp), gain(g_post_mlp), w_up, w_down)


def kernel(x, mem, positions, w_in_ret, w_in_fox, b_forget, w_mem_kv, w_out, w_up, w_down,
           g_pre_mix, g_post_mix, g_pre_mlp, g_post_mlp, g_mem):
    batch, seq, _ = x.shape
    depth = w_out.shape[0]
    assert (batch * seq) % ROW_TILE == 0 and seq % FOX_TILE == 0 and seq % RET_CHUNK == 0
    x2 = x.reshape(batch * seq, D_MODEL)
    rope = _rope_tables(positions)
    w_out_b, w_up_b, w_down_b = w_out.astype(_BF16), w_up.astype(_BF16), w_down.astype(_BF16)
    for i in range(depth):
        j = i // 2
        if i % 2 == 0:
            epi = ([("rot", 1.0)] * RET_HEADS + [("rot", RET_HEAD_DIM ** -0.5)] * RET_HEADS
                   + [("plain", 1.0)] * (2 * RET_HEADS + N_MEM_BLOCKS))
            (proj,) = _inproj(x2, g_pre_mix[i], [(w_in_ret, _layer_spec(w_in_ret.shape, j))], epi, rope=rope)
            self_out = _retention(proj, batch, seq)
            mq_offset = 4 * RET_HEADS
        else:
            qkv = 3 * FOX_WIDTH
            w = w_in_fox[j]
            w_t = jnp.swapaxes(w_in_fox, 1, 2)
            w_rest = jnp.concatenate([w[:, qkv + FOX_HEADS:],
                                      jnp.pad(w[:, qkv:qkv + FOX_HEADS], ((0, 0), (0, LANES - FOX_HEADS)))], axis=1)
            npair = FOX_HEADS // 2
            epi = ([("plain", FOX_HEAD_DIM ** -0.5 * LOG2E)] * npair + [("plain", 1.0)] * (2 * npair + N_MEM_BLOCKS))
            weights = [(w_t, _layer_spec(w_t.shape, j, rows=qkv)), (w_rest, _const_spec(w_rest.shape))]
            proj, flog = _inproj(x2, g_pre_mix[i], weights, epi, n_flog=1, transposed=(0,))
            self_out = _fox(proj, flog, b_forget[j], batch, seq)
            mq_offset = 3 * npair
        mem_out = _memattn(proj, mq_offset, mem, g_mem[i], w_mem_kv, i, batch, seq)
        x2 = _tail(x2, self_out, mem_out, i, w_out_b, g_post_mix[i], g_pre_mlp[i], g_post_mlp[i], w_up_b, w_down_b)
    return x2.reshape(batch, seq, D_MODEL)
```

```python
import functools
import math

import jax
import jax.numpy as jnp
from jax import lax
from jax.experimental import pallas as pl
from jax.experimental.pallas import tpu as pltpu

D_MODEL = 1024
MEM_HEADS = 4
MEM_HEAD_DIM = 64
MEM_WIDTH = MEM_HEADS * MEM_HEAD_DIM
RET_HEAD_DIM = 128
RET_HEADS = 6
RET_WIDTH = RET_HEADS * RET_HEAD_DIM
RET_CHUNK = 128
ROPE_BASE = 10000.0
FOX_HEAD_DIM = 64
FOX_HEADS = 12
FOX_WIDTH = FOX_HEADS * FOX_HEAD_DIM
D_FF = 4 * D_MODEL
EPS = 1e-6

LANES = 128
MXU_COLS = 256
VMEM_LIMIT_BYTES = 56 * 1024 * 1024
LOG2E = 1.4426950408889634
NEG_BIG = -1e30

ROW_TILE = 1024
ROW_GROUPS = 2
FF_CHUNK = 512
FOX_TILE = 256
CUM_TILE = 256
MEM_ROWS = 256

N_SELF_BLOCKS = RET_WIDTH // LANES
N_MEM_BLOCKS = MEM_WIDTH // LANES

_NT = (((1,), (1,)), ((), ()))
_F32 = jnp.float32
_BF16 = jnp.bfloat16


def _rms_scale(x):
    return lax.rsqrt(jnp.mean(x * x, axis=-1, keepdims=True) + EPS)


def _const_spec(shape):
    nd = len(shape)
    return pl.BlockSpec(shape, lambda *_: (0,) * nd, pipeline_mode=pl.Buffered(1))


ROPE_PACK = 256


def _rope_tables(pos_ref, inv_ref, part):
    ang = pos_ref[pl.ds(part * ROPE_PACK, ROPE_PACK), :] * inv_ref[...]
    c, s = jnp.cos(ang), jnp.sin(ang)
    half = RET_HEAD_DIM // 2
    c_sw, s_sw = pltpu.roll(c, half, 1), pltpu.roll(s, half, 1)
    low = lax.broadcasted_iota(jnp.int32, (1, LANES), 1) < half
    return (jnp.concatenate([jnp.where(low, c, c_sw), jnp.where(low, c_sw, c)], axis=0),
            jnp.concatenate([jnp.where(low, -s, s_sw), jnp.where(low, -s_sw, s)], axis=0))


def _rope_inputs(positions):
    t = positions.size
    half = RET_HEAD_DIM // 2
    inv_freq = ROPE_BASE ** (-jnp.arange(0, RET_HEAD_DIM, 2, dtype=_F32) / RET_HEAD_DIM)
    pos2 = positions.reshape(t // (2 * ROPE_PACK), 2, ROPE_PACK).transpose(0, 2, 1).reshape(t // 2, 2)
    return jnp.repeat(pos2.astype(_F32), half, axis=1), jnp.tile(inv_freq, 2).reshape(1, LANES)


CAST_COLS = 512


def _cast_weights(w_refs, w_scr, transposed=()):
    @pl.when(pl.program_id(0) == 0)
    def _():
        base = 0
        for n, w_ref in enumerate(w_refs):
            w2 = w_ref.at[0] if len(w_ref.shape) == 3 else w_ref
            width = w2.shape[0] if n in transposed else w2.shape[1]
            for c0 in range(0, width, CAST_COLS):
                c1 = min(c0 + CAST_COLS, width)
                block = w2[c0:c1, :].T if n in transposed else w2[:, c0:c1]
                w_scr[:, base + c0:base + c1] = block.astype(w_scr.dtype)
            base += width


def _inproj_kernel(*refs, epilogues, has_rope, n_flog, n_weights, transposed):
    x_ref, g_ref = refs[:2]
    w_refs = refs[2:2 + n_weights]
    rest = list(refs[2 + n_weights:])
    pos_ref, inv_ref = (rest.pop(0), rest.pop(0)) if has_rope else (None, None)
    o_ref = rest.pop(0)
    flog_ref = rest.pop(0) if n_flog else None
    w_scr = rest.pop(0)
    _cast_weights(w_refs, w_scr, transposed)
    nb = len(epilogues)
    sub = ROW_TILE // ROW_GROUPS
    groups = [pl.ds(part * sub, sub) for part in range(ROW_GROUPS)]

    def pre_norm(rows):
        x = x_ref[rows, :]
        return (x * _rms_scale(x) * g_ref[...]).astype(_BF16)

    def finish(rows, j, a, rope):
        kind, scale = epilogues[j]
        if kind == "rot":
            a = a * rope[0] + pltpu.roll(a, RET_HEAD_DIM // 2, 1) * rope[1]
        if scale != 1.0:
            a = a * scale
        o_ref[j, rows, :] = a.astype(o_ref.dtype)

    hs = [pre_norm(rows) for rows in groups]
    ropes = [_rope_tables(pos_ref, inv_ref, part) if has_rope else None for part in range(ROW_GROUPS)]
    for rows, h, rope in zip(groups, hs, ropes):
        for jb in range(nb // 2):
            acc = jnp.dot(h, w_scr[:, jb * MXU_COLS:(jb + 1) * MXU_COLS], preferred_element_type=_F32)
            finish(rows, 2 * jb, acc[:, :LANES], rope)
            finish(rows, 2 * jb + 1, acc[:, LANES:], rope)
        if n_flog:
            base = nb * LANES
            flog_ref[rows, :] = jnp.dot(h, w_scr[:, base:base + LANES], preferred_element_type=_F32)


def _inproj(x2, g, weights, epilogues, rope=None, n_flog=0, transposed=()):
    t = x2.shape[0]
    nb = len(epilogues)
    ncols = sum(spec.block_shape[-2 if n in transposed else -1] for n, (_, spec) in enumerate(weights))
    assert ncols == (nb + n_flog) * LANES
    in_specs = [pl.BlockSpec((ROW_TILE, D_MODEL), lambda i: (i, 0)), _const_spec((1, D_MODEL))]
    in_specs += [spec for _, spec in weights]
    args = [x2, g.reshape(1, D_MODEL)] + [w for w, _ in weights]
    if rope is not None:
        in_specs += [pl.BlockSpec((ROW_TILE // 2, LANES), lambda i: (i, 0)), _const_spec((1, LANES))]
        args += list(rope)
    out_shape = [jax.ShapeDtypeStruct((nb, t, LANES), _BF16)]
    out_specs = [pl.BlockSpec((nb, ROW_TILE, LANES), lambda i: (0, i, 0))]
    if n_flog:
        out_shape.append(jax.ShapeDtypeStruct((t, LANES), _F32))
        out_specs.append(pl.BlockSpec((ROW_TILE, LANES), lambda i: (i, 0)))
    return pl.pallas_call(
        functools.partial(_inproj_kernel, epilogues=tuple(epilogues), has_rope=rope is not None, n_flog=n_flog,
                          n_weights=len(weights), transposed=tuple(transposed)),
        out_shape=out_shape,
        grid=(t // ROW_TILE,),
        in_specs=in_specs,
        out_specs=out_specs,
        scratch_shapes=[pltpu.VMEM((D_MODEL, ncols), _BF16)],
        compiler_params=pltpu.CompilerParams(dimension_semantics=("arbitrary",), vmem_limit_bytes=VMEM_LIMIT_BYTES),
        name="inproj_rope" if rope is not None else "inproj",
    )(*args)


def _layer_spec(shape, layer, rows=None):
    return pl.BlockSpec((1, shape[1] if rows is None else rows, shape[2]), lambda *_: (layer, 0, 0),
                        pipeline_mode=pl.Buffered(1))


def _retention_kernel(q_ref, k_ref, v_ref, gate_ref, dmask_ref, qd_ref, kd_ref, cd_ref, o_ref, *, n_chunks):
    c = RET_CHUNK
    dmask = dmask_ref[0]
    qd = qd_ref[0]
    kd = kd_ref[0]
    cd = cd_ref[0]

    inner, kv = [], []
    for n in range(n_chunks):
        rows = pl.ds(n * c, c)
        qc, kc, vc = q_ref[0, rows, :], k_ref[0, rows, :], v_ref[0, rows, :]
        inner.append((lax.dot_general(qc, kc, _NT, preferred_element_type=_F32) * dmask).astype(_BF16))
        k_scaled_t = (kc.astype(_F32) * kd).T.astype(_BF16)
        kv.append(jnp.dot(k_scaled_t, vc, preferred_element_type=_F32))

    state = jnp.zeros((RET_HEAD_DIM, RET_HEAD_DIM), _F32)
    for n in range(n_chunks):
        rows = pl.ds(n * c, c)
        q_scaled = (q_ref[0, rows, :].astype(_F32) * qd).astype(_BF16)
        lhs = jnp.concatenate([inner[n], q_scaled], axis=1)
        rhs = jnp.concatenate([v_ref[0, rows, :], state.astype(_BF16)], axis=0)
        out = jnp.dot(lhs, rhs, preferred_element_type=_F32)
        state = state * cd + kv[n]
        r = out * _rms_scale(out)
        g = gate_ref[0, rows, :].astype(_F32)
        o_ref[0, rows, :] = (r * (g * jax.nn.sigmoid(g))).astype(o_ref.dtype)


def _retention_tables():
    c = RET_CHUNK
    log_gamma = jnp.log(1.0 - 2.0 ** (-5.0 - jnp.arange(RET_HEADS, dtype=_F32)))
    idx = jnp.arange(c, dtype=_F32)
    rel = idx[:, None] - idx[None, :]
    dmask = jnp.where(rel[None] >= 0, jnp.exp(jnp.maximum(rel, 0.0)[None] * log_gamma[:, None, None]), 0.0)
    q_decay = jnp.exp((idx[None, :] + 1.0) * log_gamma[:, None])
    k_decay = jnp.exp((c - 1.0 - idx[None, :]) * log_gamma[:, None])
    chunk_decay = jnp.exp(c * log_gamma)
    full = (RET_HEADS, c, RET_HEAD_DIM)
    return (dmask,
            jnp.broadcast_to(q_decay[:, :, None], full),
            jnp.broadcast_to(k_decay[:, :, None], full),
            jnp.broadcast_to(chunk_decay[:, None, None], full))


def _retention(proj, batch, seq):
    nh = RET_HEADS
    p4 = proj.reshape(proj.shape[0], batch, seq, LANES)

    def slab(group):
        return pl.BlockSpec((nh, 1, seq, LANES), lambda b: (group, b, 0, 0))

    def table():
        return _const_spec((nh, RET_CHUNK, RET_HEAD_DIM))

    def body(q_ref, k_ref, v_ref, gate_ref, dm, qd, kd, cd, o_ref):
        for h in range(nh):
            one = pl.ds(h, 1)
            _retention_kernel(q_ref.at[h], k_ref.at[h], v_ref.at[h], gate_ref.at[h],
                              dm.at[one], qd.at[one], kd.at[one], cd.at[one], o_ref.at[h], n_chunks=seq // RET_CHUNK)

    out = pl.pallas_call(
        body,
        out_shape=jax.ShapeDtypeStruct((nh, batch, seq, LANES), _BF16),
        grid=(batch,),
        in_specs=[slab(0), slab(1), slab(2), slab(3), table(), table(), table(), table()],
        out_specs=pl.BlockSpec((nh, 1, seq, LANES), lambda b: (0, b, 0, 0)),
        compiler_params=pltpu.CompilerParams(dimension_semantics=("arbitrary",), vmem_limit_bytes=VMEM_LIMIT_BYTES),
        name="retention",
    )(p4, p4, p4, p4, *_retention_tables())
    return out.reshape(nh, batch * seq, LANES)


def _split3(x):
    hi = x.astype(_BF16)
    r1 = x - hi.astype(_F32)
    mid = r1.astype(_BF16)
    lo = (r1 - mid.astype(_F32)).astype(_BF16)
    return hi, mid, lo


C_PART_STRIDE = 16
ONES_LANE = LANES - 1


def _fox_kernel(q_ref, k_ref, v_ref, flog_ref, bias_ref, sel_ref, tri_ref, o_ref,
                cpack_ref, qa_ref, ka_ref, va_ref, *, seq):
    pair = pl.program_id(1)
    lane = lax.broadcasted_iota(jnp.int32, (1, LANES), 1)
    low_half = lane < FOX_HEAD_DIM
    keep = (low_half, jnp.logical_not(low_half))
    sum_lane = (FOX_HEAD_DIM, 0)

    @pl.when(pair == 0)
    def _():
        tri = tri_ref[...]
        head_lanes = lane < C_PART_STRIDE
        carry = jnp.zeros((1, LANES), _F32)
        for i in range(seq // CUM_TILE):
            rows = pl.ds(i * CUM_TILE, CUM_TILE)
            z = flog_ref[0, rows, :] + bias_ref[...]
            log_f = jnp.minimum(z, 0.0) - jnp.log1p(jnp.exp(-jnp.abs(z)))
            hi, mid, lo = _split3(log_f)
            cs = (jnp.dot(tri, hi, preferred_element_type=_F32)
                  + jnp.dot(tri, mid, preferred_element_type=_F32)
                  + jnp.dot(tri, lo, preferred_element_type=_F32)) + carry
            carry = cs[CUM_TILE - 1:CUM_TILE, :]
            parts = [jnp.where(head_lanes, part.astype(_F32), 0.0) for part in _split3(cs * LOG2E)]
            packed = (parts[0] + pltpu.roll(parts[1], C_PART_STRIDE, 1) + pltpu.roll(parts[2], 2 * C_PART_STRIDE, 1))
            cpack_ref[rows, :] = jnp.where(lane == ONES_LANE, 1.0, packed).astype(_BF16)

    blk = 512
    for r in range(seq // blk):
        rows = pl.ds(r * blk, blk)
        extras = jnp.dot(cpack_ref[rows, :], sel_ref[0], preferred_element_type=_F32).astype(_BF16)
        for e in range(2):
            qa_ref[e, rows, :] = jnp.where(keep[e], q_ref[0, rows, :], extras[:, (2 * e) * LANES:(2 * e + 1) * LANES])
            ka_ref[e, rows, :] = jnp.where(keep[e], k_ref[0, rows, :], extras[:, (2 * e + 1) * LANES:(2 * e + 2) * LANES])
            ones_col = jnp.where(lane == sum_lane[e], 1.0, 0.0).astype(_BF16)
            va_ref[e, rows, :] = jnp.where(keep[e], v_ref[0, rows, :], ones_col)

    t = FOX_TILE
    row_id = lax.broadcasted_iota(jnp.int32, (t, t), 0)
    col_id = lax.broadcasted_iota(jnp.int32, (t, t), 1)
    causal = col_id <= row_id

    for qi in range(seq // t):
        q_rows = pl.ds(qi * t, t)
        acc = []
        for e in range(2):
            qa = qa_ref[e, q_rows, :]
            m_run = jnp.full((t, 1), NEG_BIG, _F32)
            a = jnp.zeros((t, LANES), _F32)
            for kj in range(qi + 1):
                k_rows = pl.ds(kj * t, t)
                s = lax.dot_general(qa, ka_ref[e, k_rows, :], _NT, preferred_element_type=_F32)
                if kj == qi:
                    s = jnp.where(causal, s, NEG_BIG)
                m_new = jnp.maximum(m_run, jnp.max(s, axis=-1, keepdims=True))
                p = jnp.exp2(s - m_new).astype(_BF16)
                a = a * jnp.exp2(m_run - m_new) + jnp.dot(p, va_ref[e, k_rows, :], preferred_element_type=_F32)
                m_run = m_new
            acc.append(a / a[:, sum_lane[e]:sum_lane[e] + 1])
        o_ref[0, q_rows, :] = jnp.where(low_half, acc[0], acc[1]).astype(o_ref.dtype)


def _fox_selectors():
    import numpy as np
    npair = FOX_HEADS // 2
    sel = np.zeros((npair, LANES, 4 * LANES), np.float32)
    for h in range(FOX_HEADS):
        p, e = divmod(h, 2)
        base = FOX_HEAD_DIM if e == 0 else 0
        qcol, kcol = (2 * e) * LANES + base, (2 * e + 1) * LANES + base
        for part in range(3):
            sel[p, part * C_PART_STRIDE + h, qcol + part] = 1.0
            sel[p, ONES_LANE, kcol + part] = 1.0
            sel[p, ONES_LANE, qcol + 3 + part] = 1.0
            sel[p, part * C_PART_STRIDE + h, kcol + 3 + part] = -1.0
    return jnp.asarray(sel, _BF16)


def _fox(proj, flog, b_forget, batch, seq):
    npair = FOX_HEADS // 2
    p4 = proj.reshape(proj.shape[0], batch, seq, LANES)
    flog3 = flog.reshape(batch, seq, LANES)
    bias = jnp.zeros((1, LANES), _F32).at[0, :FOX_HEADS].set(b_forget.astype(_F32))
    tri = jnp.tril(jnp.ones((CUM_TILE, CUM_TILE), _F32)).astype(_BF16)

    def slab(offset):
        return pl.BlockSpec((1, 1, seq, LANES), lambda b, p: (offset + p, b, 0, 0))

    def body(q_ref, k_ref, v_ref, flog_ref, bias_ref, sel_ref, tri_ref, o_ref, *scratch):
        _fox_kernel(q_ref.at[0], k_ref.at[0], v_ref.at[0], flog_ref, bias_ref, sel_ref, tri_ref,
                    o_ref.at[0], *scratch, seq=seq)

    out = pl.pallas_call(
        body,
        out_shape=jax.ShapeDtypeStruct((npair, batch, seq, LANES), _BF16),
        grid=(batch, npair),
        in_specs=[slab(0), slab(npair), slab(2 * npair),
                  pl.BlockSpec((1, seq, LANES), lambda b, p: (b, 0, 0)),
                  _const_spec((1, LANES)),
                  pl.BlockSpec((1, LANES, 4 * LANES), lambda b, p: (p, 0, 0)),
                  _const_spec((CUM_TILE, CUM_TILE))],
        out_specs=pl.BlockSpec((1, 1, seq, LANES), lambda b, p: (p, b, 0, 0)),
        scratch_shapes=[pltpu.VMEM((seq, LANES), _BF16),
                        pltpu.VMEM((2, seq, LANES), _BF16),
                        pltpu.VMEM((2, seq, LANES), _BF16),
                        pltpu.VMEM((2, seq, LANES), _BF16)],
        compiler_params=pltpu.CompilerParams(dimension_semantics=("arbitrary", "arbitrary"),
                                             vmem_limit_bytes=VMEM_LIMIT_BYTES),
        name="fox_attention",
    )(p4, p4, p4, flog3, bias, _fox_selectors(), tri)
    return out.reshape(npair, batch * seq, LANES)


def _memattn_kernel(mem_ref, g_ref, w_ref, mq_ref, o_ref, w_scr, *, seq):
    _cast_weights([w_ref], w_scr)
    mem = mem_ref[0]
    kvn = (mem * _rms_scale(mem) * g_ref[...]).astype(_BF16)
    kv = jnp.dot(kvn, w_scr[...], preferred_element_type=_F32)
    lane = lax.broadcasted_iota(jnp.int32, (1, LANES), 1)
    low_half = lane < MEM_HEAD_DIM
    scale = MEM_HEAD_DIM ** -0.5
    for p in range(N_MEM_BLOCKS):
        mk = (kv[:, p * LANES:(p + 1) * LANES] * scale).astype(_BF16)
        mv = kv[:, MEM_WIDTH + p * LANES:MEM_WIDTH + (p + 1) * LANES].astype(_BF16)
        heads = []
        for e in range(2):
            keep = low_half if e == 0 else jnp.logical_not(low_half)
            sum_lane = MEM_HEAD_DIM if e == 0 else 0
            ones_col = jnp.where(lane == sum_lane, 1.0, 0.0).astype(_BF16)
            heads.append((jnp.where(keep, mk, jnp.zeros_like(mk)), jnp.where(keep, mv, ones_col), sum_lane))
        row_tiles = [pl.ds(i * MEM_ROWS, MEM_ROWS) for i in range(seq // MEM_ROWS)]
        scores = [[lax.dot_general(mq_ref[p, rows, :], mk_e, _NT, preferred_element_type=_F32)
                   for mk_e, _, _ in heads] for rows in row_tiles]
        probs = [[jnp.exp(s - jnp.max(s, axis=-1, keepdims=True)).astype(_BF16) for s in pair] for pair in scores]
        for rows, pair in zip(row_tiles, probs):
            outs = []
            for prob, (_, mv_e, sum_lane) in zip(pair, heads):
                out = jnp.dot(prob, mv_e, preferred_element_type=_F32)
                outs.append(out / out[:, sum_lane:sum_lane + 1])
            o_ref[p, rows, :] = jnp.where(low_half, outs[0], outs[1]).astype(o_ref.dtype)


def _memattn(proj, mq_offset, mem, g_mem, w_mem_kv, layer, batch, seq):
    p4 = proj.reshape(proj.shape[0], batch, seq, LANES)
    mem_len = mem.shape[1]

    def body(mem_ref, g_ref, w_ref, mq_ref, o_ref, w_scr):
        _memattn_kernel(mem_ref, g_ref, w_ref, mq_ref.at[:, 0], o_ref.at[:, 0], w_scr, seq=seq)

    out = pl.pallas_call(
        body,
        out_shape=jax.ShapeDtypeStruct((N_MEM_BLOCKS, batch, seq, LANES), _BF16),
        grid=(batch,),
        in_specs=[pl.BlockSpec((1, mem_len, D_MODEL), lambda b: (b, 0, 0)),
                  _const_spec((1, D_MODEL)),
                  _layer_spec(w_mem_kv.shape, layer),
                  pl.BlockSpec((N_MEM_BLOCKS, 1, seq, LANES), lambda b: (mq_offset // N_MEM_BLOCKS, b, 0, 0))],
        out_specs=pl.BlockSpec((N_MEM_BLOCKS, 1, seq, LANES), lambda b: (0, b, 0, 0)),
        scratch_shapes=[pltpu.VMEM((D_MODEL, 2 * MEM_WIDTH), _BF16)],
        compiler_params=pltpu.CompilerParams(dimension_semantics=("arbitrary",), vmem_limit_bytes=VMEM_LIMIT_BYTES),
        name="memory_attention",
    )(mem, g_mem.reshape(1, D_MODEL), w_mem_kv, p4)
    return out.reshape(N_MEM_BLOCKS, batch * seq, LANES)


def _tail_kernel(x_ref, so_ref, mo_ref, wo_ref, g1_ref, g2_ref, g3_ref, wup_ref, wdn_ref, o_ref):
    sub = ROW_TILE // ROW_GROUPS
    groups = [pl.ds(part * sub, sub) for part in range(ROW_GROUPS)]

    def out_proj(rows):
        mixed = jnp.concatenate([so_ref[j, rows, :] for j in range(N_SELF_BLOCKS)]
                                + [mo_ref[j, rows, :] for j in range(N_MEM_BLOCKS)], axis=1)
        return jnp.dot(mixed, wo_ref[0], preferred_element_type=_F32)

    def mid_norms(rows, y):
        x1 = x_ref[rows, :] + y * _rms_scale(y) * g1_ref[...]
        return x1, (x1 * _rms_scale(x1) * g2_ref[...]).astype(_BF16)

    def mlp(h):
        acc = jnp.zeros((sub, D_MODEL), _F32)
        for c in range(D_FF // FF_CHUNK):
            cols = slice(c * FF_CHUNK, (c + 1) * FF_CHUNK)
            u = jnp.maximum(jnp.dot(h, wup_ref[0, :, cols], preferred_element_type=_F32), 0.0)
            acc = acc + jnp.dot((u * u).astype(_BF16), wdn_ref[0, cols, :], preferred_element_type=_F32)
        return acc

    ys = [out_proj(rows) for rows in groups]
    mids = [mid_norms(rows, y) for rows, y in zip(groups, ys)]
    for rows, (x1, h) in zip(groups, mids):
        acc = mlp(h)
        o_ref[rows, :] = x1 + acc * _rms_scale(acc) * g3_ref[...]


def _tail(x2, self_out, mem_out, layer, w_out, g_post_mix, g_pre_mlp, g_post_mlp, w_up, w_down):
    t = x2.shape[0]
    row = lambda: pl.BlockSpec((ROW_TILE, D_MODEL), lambda i: (i, 0))
    gain = lambda g: g.reshape(1, D_MODEL)
    return pl.pallas_call(
        _tail_kernel,
        out_shape=jax.ShapeDtypeStruct((t, D_MODEL), _F32),
        grid=(t // ROW_TILE,),
        in_specs=[row(),
                  pl.BlockSpec((N_SELF_BLOCKS, ROW_TILE, LANES), lambda i: (0, i, 0)),
                  pl.BlockSpec((N_MEM_BLOCKS, ROW_TILE, LANES), lambda i: (0, i, 0)),
                  _layer_spec(w_out.shape, layer),
                  _const_spec((1, D_MODEL)), _const_spec((1, D_MODEL)), _const_spec((1, D_MODEL)),
                  _layer_spec(w_up.shape, layer), _layer_spec(w_down.shape, layer)],
        out_specs=row(),
        compiler_params=pltpu.CompilerParams(dimension_semantics=("arbitrary",), vmem_limit_bytes=VMEM_LIMIT_BYTES),
        name="outproj_mlp",
    )(x2, self_out, mem_out, w_out, gain(g_post_mix), gain(g_pre_mlp), gain(g_post_mlp), w_up, w_down)


def kernel(x, mem, positions, w_in_ret, w_in_fox, b_forget, w_mem_kv, w_out, w_up, w_down,
           g_pre_mix, g_post_mix, g_pre_mlp, g_post_mlp, g_mem):
    batch, seq, _ = x.shape
    depth = w_out.shape[0]
    assert ROW_TILE == 2 * ROPE_PACK * ROW_GROUPS and (batch * seq) % ROW_TILE == 0 and seq % FOX_TILE == 0 and seq % RET_CHUNK == 0
    x2 = x.reshape(batch * seq, D_MODEL)
    rope = _rope_inputs(positions)
    w_out_b, w_up_b, w_down_b = w_out.astype(_BF16), w_up.astype(_BF16), w_down.astype(_BF16)
    for i in range(depth):
        j = i // 2
        if i % 2 == 0:
            epi = ([("rot", 1.0)] * RET_HEADS + [("rot", RET_HEAD_DIM ** -0.5)] * RET_HEADS
                   + [("plain", 1.0)] * (2 * RET_HEADS + N_MEM_BLOCKS))
            (proj,) = _inproj(x2, g_pre_mix[i], [(w_in_ret, _layer_spec(w_in_ret.shape, j))], epi, rope=rope)
            self_out = _retention(proj, batch, seq)
            mq_offset = 4 * RET_HEADS
        else:
            qkv = 3 * FOX_WIDTH
            w_t = jnp.swapaxes(w_in_fox, 1, 2)
            rest_t = w_t[j, qkv:, :]
            w_rest_t = jnp.concatenate([rest_t[FOX_HEADS:],
                                        jnp.pad(rest_t[:FOX_HEADS], ((0, LANES - FOX_HEADS), (0, 0)))], axis=0)
            npair = FOX_HEADS // 2
            epi = ([("plain", FOX_HEAD_DIM ** -0.5 * LOG2E)] * npair + [("plain", 1.0)] * (2 * npair + N_MEM_BLOCKS))
            weights = [(w_t, _layer_spec(w_t.shape, j, rows=qkv)), (w_rest_t, _const_spec(w_rest_t.shape))]
            proj, flog = _inproj(x2, g_pre_mix[i], weights, epi, n_flog=1, transposed=(0, 1))
            self_out = _fox(proj, flog, b_forget[j], batch, seq)
            mq_offset = 3 * npair
        mem_out = _memattn(proj, mq_offset, mem, g_mem[i], w_mem_kv, i, batch, seq)
        x2 = _tail(x2, self_out, mem_out, i, w_out_b, g_post_mix[i], g_pre_mlp[i], g_post_mlp[i], w_up_b, w_down_b)
    return x2.reshape(batch, seq, D_MODEL)
```

```python
import functools
import math

import jax
import jax.numpy as jnp
from jax import lax
from jax.experimental import pallas as pl
from jax.experimental.pallas import tpu as pltpu

D_MODEL = 1024
MEM_HEADS = 4
MEM_HEAD_DIM = 64
MEM_WIDTH = MEM_HEADS * MEM_HEAD_DIM
RET_HEAD_DIM = 128
RET_HEADS = 6
RET_WIDTH = RET_HEADS * RET_HEAD_DIM
RET_CHUNK = 128
ROPE_BASE = 10000.0
FOX_HEAD_DIM = 64
FOX_HEADS = 12
FOX_WIDTH = FOX_HEADS * FOX_HEAD_DIM
D_FF = 4 * D_MODEL
EPS = 1e-6

LANES = 128
MXU_COLS = 256
VMEM_LIMIT_BYTES = 56 * 1024 * 1024
LOG2E = 1.4426950408889634
NEG_BIG = -1e30

ROW_TILE = 1024
ROW_GROUPS = 2
FF_CHUNK = 512
FOX_TILE = 256
CUM_TILE = 256
MEM_ROWS = 256

N_SELF_BLOCKS = RET_WIDTH // LANES
N_MEM_BLOCKS = MEM_WIDTH // LANES

_NT = (((1,), (1,)), ((), ()))
_F32 = jnp.float32
_BF16 = jnp.bfloat16


def _rms_scale(x):
    return lax.rsqrt(jnp.mean(x * x, axis=-1, keepdims=True) + EPS)


def _const_spec(shape):
    nd = len(shape)
    return pl.BlockSpec(shape, lambda *_: (0,) * nd, pipeline_mode=pl.Buffered(1))


ROPE_PACK = 256


def _rope_tables(pos_ref, inv_ref, part):
    ang = pos_ref[pl.ds(part * ROPE_PACK, ROPE_PACK), :] * inv_ref[...]
    c, s = jnp.cos(ang), jnp.sin(ang)
    half = RET_HEAD_DIM // 2
    c_sw, s_sw = pltpu.roll(c, half, 1), pltpu.roll(s, half, 1)
    low = lax.broadcasted_iota(jnp.int32, (1, LANES), 1) < half
    return (jnp.concatenate([jnp.where(low, c, c_sw), jnp.where(low, c_sw, c)], axis=0),
            jnp.concatenate([jnp.where(low, -s, s_sw), jnp.where(low, -s_sw, s)], axis=0))


def _rope_inputs(positions):
    t = positions.size
    half = RET_HEAD_DIM // 2
    inv_freq = ROPE_BASE ** (-jnp.arange(0, RET_HEAD_DIM, 2, dtype=_F32) / RET_HEAD_DIM)
    pos2 = positions.reshape(t // (2 * ROPE_PACK), 2, ROPE_PACK).transpose(0, 2, 1).reshape(t // 2, 2)
    return jnp.repeat(pos2.astype(_F32), half, axis=1), jnp.tile(inv_freq, 2).reshape(1, LANES)


CAST_COLS = 512


def _cast_weights(w_refs, w_scr, transposed=()):
    @pl.when(pl.program_id(0) == 0)
    def _():
        base = 0
        for n, w_ref in enumerate(w_refs):
            w2 = w_ref.at[0] if len(w_ref.shape) == 3 else w_ref
            width = w2.shape[0] if n in transposed else w2.shape[1]
            for c0 in range(0, width, CAST_COLS):
                c1 = min(c0 + CAST_COLS, width)
                block = w2[c0:c1, :].T if n in transposed else w2[:, c0:c1]
                w_scr[:, base + c0:base + c1] = block.astype(w_scr.dtype)
            base += width


def _inproj_kernel(*refs, epilogues, has_rope, n_flog, n_weights, transposed):
    x_ref, g_ref = refs[:2]
    w_refs = refs[2:2 + n_weights]
    rest = list(refs[2 + n_weights:])
    pos_ref, inv_ref = (rest.pop(0), rest.pop(0)) if has_rope else (None, None)
    o_ref = rest.pop(0)
    flog_ref = rest.pop(0) if n_flog else None
    w_scr = rest.pop(0)
    _cast_weights(w_refs, w_scr, transposed)
    nb = len(epilogues)
    sub = ROW_TILE // ROW_GROUPS
    groups = [pl.ds(part * sub, sub) for part in range(ROW_GROUPS)]

    def pre_norm(rows):
        x = x_ref[rows, :]
        return (x * _rms_scale(x) * g_ref[...]).astype(_BF16)

    def finish(rows, j, a, rope):
        kind, scale = epilogues[j]
        if kind == "rot":
            a = a * rope[0] + pltpu.roll(a, RET_HEAD_DIM // 2, 1) * rope[1]
        if scale != 1.0:
            a = a * scale
        o_ref[j, rows, :] = a.astype(o_ref.dtype)

    hs = [pre_norm(rows) for rows in groups]
    ropes = [_rope_tables(pos_ref, inv_ref, part) if has_rope else None for part in range(ROW_GROUPS)]
    for rows, h, rope in zip(groups, hs, ropes):
        for jb in range(nb // 2):
            acc = jnp.dot(h, w_scr[:, jb * MXU_COLS:(jb + 1) * MXU_COLS], preferred_element_type=_F32)
            finish(rows, 2 * jb, acc[:, :LANES], rope)
            finish(rows, 2 * jb + 1, acc[:, LANES:], rope)
        if n_flog:
            base = nb * LANES
            flog_ref[rows, :] = jnp.dot(h, w_scr[:, base:base + LANES], preferred_element_type=_F32)


def _inproj(x2, g, weights, epilogues, rope=None, n_flog=0, transposed=()):
    t = x2.shape[0]
    nb = len(epilogues)
    ncols = sum(spec.block_shape[-2 if n in transposed else -1] for n, (_, spec) in enumerate(weights))
    assert ncols == (nb + n_flog) * LANES
    in_specs = [pl.BlockSpec((ROW_TILE, D_MODEL), lambda i: (i, 0)), _const_spec((1, D_MODEL))]
    in_specs += [spec for _, spec in weights]
    args = [x2, g.reshape(1, D_MODEL)] + [w for w, _ in weights]
    if rope is not None:
        in_specs += [pl.BlockSpec((ROW_TILE // 2, LANES), lambda i: (i, 0)), _const_spec((1, LANES))]
        args += list(rope)
    out_shape = [jax.ShapeDtypeStruct((nb, t, LANES), _BF16)]
    out_specs = [pl.BlockSpec((nb, ROW_TILE, LANES), lambda i: (0, i, 0))]
    if n_flog:
        out_shape.append(jax.ShapeDtypeStruct((t, LANES), _F32))
        out_specs.append(pl.BlockSpec((ROW_TILE, LANES), lambda i: (i, 0)))
    return pl.pallas_call(
        functools.partial(_inproj_kernel, epilogues=tuple(epilogues), has_rope=rope is not None, n_flog=n_flog,
                          n_weights=len(weights), transposed=tuple(transposed)),
        out_shape=out_shape,
        grid=(t // ROW_TILE,),
        in_specs=in_specs,
        out_specs=out_specs,
        scratch_shapes=[pltpu.VMEM((D_MODEL, ncols), _BF16)],
        compiler_params=pltpu.CompilerParams(dimension_semantics=("arbitrary",), vmem_limit_bytes=VMEM_LIMIT_BYTES),
        name="inproj_rope" if rope is not None else "inproj",
    )(*args)


def _layer_spec(shape, layer, rows=None):
    return pl.BlockSpec((1, shape[1] if rows is None else rows, shape[2]), lambda *_: (layer, 0, 0),
                        pipeline_mode=pl.Buffered(1))


def _retention_kernel(q_ref, k_ref, v_ref, gate_ref, dmask_ref, qd_ref, kd_ref, cd_ref, o_ref, *, n_chunks):
    c = RET_CHUNK
    dmask = dmask_ref[0]
    qd = qd_ref[0]
    kd = kd_ref[0]
    cd = cd_ref[0]

    inner, kv = [], []
    for n in range(n_chunks):
        rows = pl.ds(n * c, c)
        qc, kc, vc = q_ref[0, rows, :], k_ref[0, rows, :], v_ref[0, rows, :]
        inner.append((lax.dot_general(qc, kc, _NT, preferred_element_type=_F32) * dmask).astype(_BF16))
        k_scaled_t = (kc.astype(_F32) * kd).T.astype(_BF16)
        kv.append(jnp.dot(k_scaled_t, vc, preferred_element_type=_F32))

    state = jnp.zeros((RET_HEAD_DIM, RET_HEAD_DIM), _F32)
    for n in range(n_chunks):
        rows = pl.ds(n * c, c)
        q_scaled = (q_ref[0, rows, :].astype(_F32) * qd).astype(_BF16)
        lhs = jnp.concatenate([inner[n], q_scaled], axis=1)
        rhs = jnp.concatenate([v_ref[0, rows, :], state.astype(_BF16)], axis=0)
        out = jnp.dot(lhs, rhs, preferred_element_type=_F32)
        state = state * cd + kv[n]
        r = out * _rms_scale(out)
        g = gate_ref[0, rows, :].astype(_F32)
        o_ref[0, rows, :] = (r * (g * jax.nn.sigmoid(g))).astype(o_ref.dtype)


def _retention_tables():
    c = RET_CHUNK
    log_gamma = jnp.log(1.0 - 2.0 ** (-5.0 - jnp.arange(RET_HEADS, dtype=_F32)))
    idx = jnp.arange(c, dtype=_F32)
    rel = idx[:, None] - idx[None, :]
    dmask = jnp.where(rel[None] >= 0, jnp.exp(jnp.maximum(rel, 0.0)[None] * log_gamma[:, None, None]), 0.0)
    q_decay = jnp.exp((idx[None, :] + 1.0) * log_gamma[:, None])
    k_decay = jnp.exp((c - 1.0 - idx[None, :]) * log_gamma[:, None])
    chunk_decay = jnp.exp(c * log_gamma)
    full = (RET_HEADS, c, RET_HEAD_DIM)
    return (dmask,
            jnp.broadcast_to(q_decay[:, :, None], full),
            jnp.broadcast_to(k_decay[:, :, None], full),
            jnp.broadcast_to(chunk_decay[:, None, None], full))


def _retention(proj, batch, seq):
    nh = RET_HEADS
    p4 = proj.reshape(proj.shape[0], batch, seq, LANES)

    def slab(group):
        return pl.BlockSpec((nh, 1, seq, LANES), lambda b: (group, b, 0, 0))

    def table():
        return _const_spec((nh, RET_CHUNK, RET_HEAD_DIM))

    def body(q_ref, k_ref, v_ref, gate_ref, dm, qd, kd, cd, o_ref):
        for h in range(nh):
            one = pl.ds(h, 1)
            _retention_kernel(q_ref.at[h], k_ref.at[h], v_ref.at[h], gate_ref.at[h],
                              dm.at[one], qd.at[one], kd.at[one], cd.at[one], o_ref.at[h], n_chunks=seq // RET_CHUNK)

    out = pl.pallas_call(
        body,
        out_shape=jax.ShapeDtypeStruct((nh, batch, seq, LANES), _BF16),
        grid=(batch,),
        in_specs=[slab(0), slab(1), slab(2), slab(3), table(), table(), table(), table()],
        out_specs=pl.BlockSpec((nh, 1, seq, LANES), lambda b: (0, b, 0, 0)),
        compiler_params=pltpu.CompilerParams(dimension_semantics=("arbitrary",), vmem_limit_bytes=VMEM_LIMIT_BYTES),
        name="retention",
    )(p4, p4, p4, p4, *_retention_tables())
    return out.reshape(nh, batch * seq, LANES)


def _split3(x):
    hi = x.astype(_BF16)
    r1 = x - hi.astype(_F32)
    mid = r1.astype(_BF16)
    lo = (r1 - mid.astype(_F32)).astype(_BF16)
    return hi, mid, lo


C_PART_STRIDE = 16
ONES_LANE = LANES - 1


def _fox_kernel(q_ref, k_ref, v_ref, flog_ref, bias_ref, sel_ref, tri_ref, o_ref,
                cpack_ref, qa_ref, ka_ref, va_ref, *, seq):
    pair = pl.program_id(1)
    lane = lax.broadcasted_iota(jnp.int32, (1, LANES), 1)
    low_half = lane < FOX_HEAD_DIM
    keep = (low_half, jnp.logical_not(low_half))
    sum_lane = (FOX_HEAD_DIM, 0)

    @pl.when(pair == 0)
    def _():
        tri = tri_ref[...]
        head_lanes = lane < C_PART_STRIDE
        carry = jnp.zeros((1, LANES), _F32)
        for i in range(seq // CUM_TILE):
            rows = pl.ds(i * CUM_TILE, CUM_TILE)
            z = flog_ref[0, rows, :] + bias_ref[...]
            log_f = jnp.minimum(z, 0.0) - jnp.log1p(jnp.exp(-jnp.abs(z)))
            hi, mid, lo = _split3(log_f)
            cs = (jnp.dot(tri, hi, preferred_element_type=_F32)
                  + jnp.dot(tri, mid, preferred_element_type=_F32)
                  + jnp.dot(tri, lo, preferred_element_type=_F32)) + carry
            carry = cs[CUM_TILE - 1:CUM_TILE, :]
            parts = [jnp.where(head_lanes, part.astype(_F32), 0.0) for part in _split3(cs * LOG2E)]
            packed = (parts[0] + pltpu.roll(parts[1], C_PART_STRIDE, 1) + pltpu.roll(parts[2], 2 * C_PART_STRIDE, 1))
            cpack_ref[rows, :] = jnp.where(lane == ONES_LANE, 1.0, packed).astype(_BF16)

    blk = 512
    for r in range(seq // blk):
        rows = pl.ds(r * blk, blk)
        extras = jnp.dot(cpack_ref[rows, :], sel_ref[0], preferred_element_type=_F32).astype(_BF16)
        for e in range(2):
            qa_ref[e, rows, :] = jnp.where(keep[e], q_ref[0, rows, :], extras[:, :LANES])
            ka_ref[e, rows, :] = jnp.where(keep[e], k_ref[0, rows, :], extras[:, LANES:])
            ones_col = jnp.where(lane == sum_lane[e], 1.0, 0.0).astype(_BF16)
            va_ref[e, rows, :] = jnp.where(keep[e], v_ref[0, rows, :], ones_col)

    t = FOX_TILE
    row_id = lax.broadcasted_iota(jnp.int32, (t, t), 0)
    col_id = lax.broadcasted_iota(jnp.int32, (t, t), 1)
    causal = col_id <= row_id

    for qi in range(seq // t):
        q_rows = pl.ds(qi * t, t)
        acc = []
        for e in range(2):
            qa = qa_ref[e, q_rows, :]
            m_run = jnp.full((t, 1), NEG_BIG, _F32)
            a = jnp.zeros((t, LANES), _F32)
            for kj in range(qi + 1):
                k_rows = pl.ds(kj * t, t)
                s = lax.dot_general(qa, ka_ref[e, k_rows, :], _NT, preferred_element_type=_F32)
                if kj == qi:
                    s = jnp.where(causal, s, NEG_BIG)
                m_new = jnp.maximum(m_run, jnp.max(s, axis=-1, keepdims=True))
                p = jnp.exp2(s - m_new).astype(_BF16)
                a = a * jnp.exp2(m_run - m_new) + jnp.dot(p, va_ref[e, k_rows, :], preferred_element_type=_F32)
                m_run = m_new
            acc.append(a / a[:, sum_lane[e]:sum_lane[e] + 1])
        o_ref[0, q_rows, :] = jnp.where(low_half, acc[0], acc[1]).astype(o_ref.dtype)


def _fox_selectors():
    import numpy as np
    npair = FOX_HEADS // 2
    sel = np.zeros((npair, LANES, 2 * LANES), np.float32)
    for h in range(FOX_HEADS):
        p, e = divmod(h, 2)
        base = FOX_HEAD_DIM if e == 0 else 0
        qcol, kcol = base, LANES + base
        for part in range(3):
            sel[p, part * C_PART_STRIDE + h, qcol + part] = 1.0
            sel[p, ONES_LANE, kcol + part] = 1.0
            sel[p, ONES_LANE, qcol + 3 + part] = 1.0
            sel[p, part * C_PART_STRIDE + h, kcol + 3 + part] = -1.0
    return jnp.asarray(sel, _BF16)


def _fox(proj, flog, b_forget, batch, seq):
    npair = FOX_HEADS // 2
    p4 = proj.reshape(proj.shape[0], batch, seq, LANES)
    flog3 = flog.reshape(batch, seq, LANES)
    bias = jnp.zeros((1, LANES), _F32).at[0, :FOX_HEADS].set(b_forget.astype(_F32))
    tri = jnp.tril(jnp.ones((CUM_TILE, CUM_TILE), _F32)).astype(_BF16)

    def slab(offset):
        return pl.BlockSpec((1, 1, seq, LANES), lambda b, p: (offset + p, b, 0, 0))

    def body(q_ref, k_ref, v_ref, flog_ref, bias_ref, sel_ref, tri_ref, o_ref, *scratch):
        _fox_kernel(q_ref.at[0], k_ref.at[0], v_ref.at[0], flog_ref, bias_ref, sel_ref, tri_ref,
                    o_ref.at[0], *scratch, seq=seq)

    out = pl.pallas_call(
        body,
        out_shape=jax.ShapeDtypeStruct((npair, batch, seq, LANES), _BF16),
        grid=(batch, npair),
        in_specs=[slab(0), slab(npair), slab(2 * npair),
                  pl.BlockSpec((1, seq, LANES), lambda b, p: (b, 0, 0)),
                  _const_spec((1, LANES)),
                  pl.BlockSpec((1, LANES, 2 * LANES), lambda b, p: (p, 0, 0)),
                  _const_spec((CUM_TILE, CUM_TILE))],
        out_specs=pl.BlockSpec((1, 1, seq, LANES), lambda b, p: (p, b, 0, 0)),
        scratch_shapes=[pltpu.VMEM((seq, LANES), _BF16),
                        pltpu.VMEM((2, seq, LANES), _BF16),
                        pltpu.VMEM((2, seq, LANES), _BF16),
                        pltpu.VMEM((2, seq, LANES), _BF16)],
        compiler_params=pltpu.CompilerParams(dimension_semantics=("arbitrary", "arbitrary"),
                                             vmem_limit_bytes=VMEM_LIMIT_BYTES),
        name="fox_attention",
    )(p4, p4, p4, flog3, bias, _fox_selectors(), tri)
    return out.reshape(npair, batch * seq, LANES)


def _memattn_kernel(mem_ref, g_ref, w_ref, mq_ref, o_ref, w_scr, *, seq):
    _cast_weights([w_ref], w_scr)
    mem = mem_ref[0]
    kvn = (mem * _rms_scale(mem) * g_ref[...]).astype(_BF16)
    kv = jnp.dot(kvn, w_scr[...], preferred_element_type=_F32)
    lane = lax.broadcasted_iota(jnp.int32, (1, LANES), 1)
    low_half = lane < MEM_HEAD_DIM
    scale = MEM_HEAD_DIM ** -0.5
    for p in range(N_MEM_BLOCKS):
        mk = (kv[:, p * LANES:(p + 1) * LANES] * scale).astype(_BF16)
        mv = kv[:, MEM_WIDTH + p * LANES:MEM_WIDTH + (p + 1) * LANES].astype(_BF16)
        heads = []
        for e in range(2):
            keep = low_half if e == 0 else jnp.logical_not(low_half)
            sum_lane = MEM_HEAD_DIM if e == 0 else 0
            ones_col = jnp.where(lane == sum_lane, 1.0, 0.0).astype(_BF16)
            heads.append((jnp.where(keep, mk, jnp.zeros_like(mk)), jnp.where(keep, mv, ones_col), sum_lane))
        row_tiles = [pl.ds(i * MEM_ROWS, MEM_ROWS) for i in range(seq // MEM_ROWS)]
        scores = [[lax.dot_general(mq_ref[p, rows, :], mk_e, _NT, preferred_element_type=_F32)
                   for mk_e, _, _ in heads] for rows in row_tiles]
        probs = [[jnp.exp(s - jnp.max(s, axis=-1, keepdims=True)).astype(_BF16) for s in pair] for pair in scores]
        for rows, pair in zip(row_tiles, probs):
            outs = []
            for prob, (_, mv_e, sum_lane) in zip(pair, heads):
                out = jnp.dot(prob, mv_e, preferred_element_type=_F32)
                outs.append(out / out[:, sum_lane:sum_lane + 1])
            o_ref[p, rows, :] = jnp.where(low_half, outs[0], outs[1]).astype(o_ref.dtype)


def _memattn(proj, mq_offset, mem, g_mem, w_mem_kv, layer, batch, seq):
    p4 = proj.reshape(proj.shape[0], batch, seq, LANES)
    mem_len = mem.shape[1]

    def body(mem_ref, g_ref, w_ref, mq_ref, o_ref, w_scr):
        _memattn_kernel(mem_ref, g_ref, w_ref, mq_ref.at[:, 0], o_ref.at[:, 0], w_scr, seq=seq)

    out = pl.pallas_call(
        body,
        out_shape=jax.ShapeDtypeStruct((N_MEM_BLOCKS, batch, seq, LANES), _BF16),
        grid=(batch,),
        in_specs=[pl.BlockSpec((1, mem_len, D_MODEL), lambda b: (b, 0, 0)),
                  _const_spec((1, D_MODEL)),
                  _layer_spec(w_mem_kv.shape, layer),
                  pl.BlockSpec((N_MEM_BLOCKS, 1, seq, LANES), lambda b: (mq_offset // N_MEM_BLOCKS, b, 0, 0))],
        out_specs=pl.BlockSpec((N_MEM_BLOCKS, 1, seq, LANES), lambda b: (0, b, 0, 0)),
        scratch_shapes=[pltpu.VMEM((D_MODEL, 2 * MEM_WIDTH), _BF16)],
        compiler_params=pltpu.CompilerParams(dimension_semantics=("arbitrary",), vmem_limit_bytes=VMEM_LIMIT_BYTES),
        name="memory_attention",
    )(mem, g_mem.reshape(1, D_MODEL), w_mem_kv, p4)
    return out.reshape(N_MEM_BLOCKS, batch * seq, LANES)


def _tail_kernel(x_ref, so_ref, mo_ref, wo_ref, g1_ref, g2_ref, g3_ref, wup_ref, wdn_ref, o_ref):
    sub = ROW_TILE // ROW_GROUPS
    groups = [pl.ds(part * sub, sub) for part in range(ROW_GROUPS)]

    def out_proj(rows):
        mixed = jnp.concatenate([so_ref[j, rows, :] for j in range(N_SELF_BLOCKS)]
                                + [mo_ref[j, rows, :] for j in range(N_MEM_BLOCKS)], axis=1)
        return jnp.dot(mixed, wo_ref[0], preferred_element_type=_F32)

    def mid_norms(rows, y):
        x1 = x_ref[rows, :] + y * _rms_scale(y) * g1_ref[...]
        return x1, (x1 * _rms_scale(x1) * g2_ref[...]).astype(_BF16)

    def mlp(h):
        acc = jnp.zeros((sub, D_MODEL), _F32)
        for c in range(D_FF // FF_CHUNK):
            cols = slice(c * FF_CHUNK, (c + 1) * FF_CHUNK)
            u = jnp.maximum(jnp.dot(h, wup_ref[0, :, cols], preferred_element_type=_F32), 0.0)
            acc = acc + jnp.dot((u * u).astype(_BF16), wdn_ref[0, cols, :], preferred_element_type=_F32)
        return acc

    ys = [out_proj(rows) for rows in groups]
    mids = [mid_norms(rows, y) for rows, y in zip(groups, ys)]
    for rows, (x1, h) in zip(groups, mids):
        acc = mlp(h)
        o_ref[rows, :] = x1 + acc * _rms_scale(acc) * g3_ref[...]


def _tail(x2, self_out, mem_out, layer, w_out, g_post_mix, g_pre_mlp, g_post_mlp, w_up, w_down):
    t = x2.shape[0]
    row = lambda: pl.BlockSpec((ROW_TILE, D_MODEL), lambda i: (i, 0))
    gain = lambda g: g.reshape(1, D_MODEL)
    return pl.pallas_call(
        _tail_kernel,
        out_shape=jax.ShapeDtypeStruct((t, D_MODEL), _F32),
        grid=(t // ROW_TILE,),
        in_specs=[row(),
                  pl.BlockSpec((N_SELF_BLOCKS, ROW_TILE, LANES), lambda i: (0, i, 0)),
                  pl.BlockSpec((N_MEM_BLOCKS, ROW_TILE, LANES), lambda i: (0, i, 0)),
                  _layer_spec(w_out.shape, layer),
                  _const_spec((1, D_MODEL)), _const_spec((1, D_MODEL)), _const_spec((1, D_MODEL)),
                  _layer_spec(w_up.shape, layer), _layer_spec(w_down.shape, layer)],
        out_specs=row(),
        compiler_params=pltpu.CompilerParams(dimension_semantics=("arbitrary",), vmem_limit_bytes=VMEM_LIMIT_BYTES),
        name="outproj_mlp",
    )(x2, self_out, mem_out, w_out, gain(g_post_mix), gain(g_pre_mlp), gain(g_post_mlp), w_up, w_down)


def kernel(x, mem, positions, w_in_ret, w_in_fox, b_forget, w_mem_kv, w_out, w_up, w_down,
           g_pre_mix, g_post_mix, g_pre_mlp, g_post_mlp, g_mem):
    batch, seq, _ = x.shape
    depth = w_out.shape[0]
    assert ROW_TILE == 2 * ROPE_PACK * ROW_GROUPS and (batch * seq) % ROW_TILE == 0 and seq % FOX_TILE == 0 and seq % RET_CHUNK == 0
    x2 = x.reshape(batch * seq, D_MODEL)
    rope = _rope_inputs(positions)
    w_out_b, w_up_b, w_down_b = w_out.astype(_BF16), w_up.astype(_BF16), w_down.astype(_BF16)
    for i in range(depth):
        j = i // 2
        if i % 2 == 0:
            epi = ([("rot", 1.0)] * RET_HEADS + [("rot", RET_HEAD_DIM ** -0.5)] * RET_HEADS
                   + [("plain", 1.0)] * (2 * RET_HEADS + N_MEM_BLOCKS))
            (proj,) = _inproj(x2, g_pre_mix[i], [(w_in_ret, _layer_spec(w_in_ret.shape, j))], epi, rope=rope)
            self_out = _retention(proj, batch, seq)
            mq_offset = 4 * RET_HEADS
        else:
            qkv = 3 * FOX_WIDTH
            w_t = jnp.swapaxes(w_in_fox, 1, 2)
            rest_t = w_t[j, qkv:, :]
            w_rest_t = jnp.concatenate([rest_t[FOX_HEADS:],
                                        jnp.pad(rest_t[:FOX_HEADS], ((0, LANES - FOX_HEADS), (0, 0)))], axis=0)
            npair = FOX_HEADS // 2
            epi = ([("plain", FOX_HEAD_DIM ** -0.5 * LOG2E)] * npair + [("plain", 1.0)] * (2 * npair + N_MEM_BLOCKS))
            weights = [(w_t, _layer_spec(w_t.shape, j, rows=qkv)), (w_rest_t, _const_spec(w_rest_t.shape))]
            proj, flog = _inproj(x2, g_pre_mix[i], weights, epi, n_flog=1, transposed=(0, 1))
            self_out = _fox(proj, flog, b_forget[j], batch, seq)
            mq_offset = 3 * npair
        mem_out = _memattn(proj, mq_offset, mem, g_mem[i], w_mem_kv, i, batch, seq)
        x2 = _tail(x2, self_out, mem_out, i, w_out_b, g_post_mix[i], g_pre_mlp[i], g_post_mlp[i], w_up_b, w_down_b)
    return x2.reshape(batch, seq, D_MODEL)
```

```python
import functools
import math

import jax
import jax.numpy as jnp
from jax import lax
from jax.experimental import pallas as pl
from jax.experimental.pallas import tpu as pltpu

D_MODEL = 1024
MEM_HEADS = 4
MEM_HEAD_DIM = 64
MEM_WIDTH = MEM_HEADS * MEM_HEAD_DIM
RET_HEAD_DIM = 128
RET_HEADS = 6
RET_WIDTH = RET_HEADS * RET_HEAD_DIM
RET_CHUNK = 128
ROPE_BASE = 10000.0
FOX_HEAD_DIM = 64
FOX_HEADS = 12
FOX_WIDTH = FOX_HEADS * FOX_HEAD_DIM
D_FF = 4 * D_MODEL
EPS = 1e-6

LANES = 128
MXU_COLS = 256
VMEM_LIMIT_BYTES = 56 * 1024 * 1024
LOG2E = 1.4426950408889634
NEG_BIG = -1e30

ROW_TILE = 1024
ROW_GROUPS = 2
FF_CHUNK = 512
FOX_TILE = 256
FOX_PAIRS_PER_STEP = 1
CUM_TILE = 256
MEM_ROWS = 256

N_SELF_BLOCKS = RET_WIDTH // LANES
N_MEM_BLOCKS = MEM_WIDTH // LANES

_NT = (((1,), (1,)), ((), ()))
_F32 = jnp.float32
_BF16 = jnp.bfloat16


def _rms_scale(x):
    return lax.rsqrt(jnp.mean(x * x, axis=-1, keepdims=True) + EPS)


def _const_spec(shape):
    nd = len(shape)
    return pl.BlockSpec(shape, lambda *_: (0,) * nd, pipeline_mode=pl.Buffered(1))


ROPE_PACK = 256


def _rope_tables(pos_ref, inv_ref, part):
    ang = pos_ref[pl.ds(part * ROPE_PACK, ROPE_PACK), :] * inv_ref[...]
    c, s = jnp.cos(ang), jnp.sin(ang)
    half = RET_HEAD_DIM // 2
    c_sw, s_sw = pltpu.roll(c, half, 1), pltpu.roll(s, half, 1)
    low = lax.broadcasted_iota(jnp.int32, (1, LANES), 1) < half
    return (jnp.concatenate([jnp.where(low, c, c_sw), jnp.where(low, c_sw, c)], axis=0),
            jnp.concatenate([jnp.where(low, -s, s_sw), jnp.where(low, -s_sw, s)], axis=0))


def _rope_inputs(positions):
    t = positions.size
    half = RET_HEAD_DIM // 2
    inv_freq = ROPE_BASE ** (-jnp.arange(0, RET_HEAD_DIM, 2, dtype=_F32) / RET_HEAD_DIM)
    pos2 = positions.reshape(t // (2 * ROPE_PACK), 2, ROPE_PACK).transpose(0, 2, 1).reshape(t // 2, 2)
    return jnp.repeat(pos2.astype(_F32), half, axis=1), jnp.tile(inv_freq, 2).reshape(1, LANES)


CAST_COLS = 512


def _cast_weights(w_refs, w_scr, transposed=()):
    @pl.when(pl.program_id(0) == 0)
    def _():
        base = 0
        for n, w_ref in enumerate(w_refs):
            w2 = w_ref.at[0] if len(w_ref.shape) == 3 else w_ref
            width = w2.shape[0] if n in transposed else w2.shape[1]
            for c0 in range(0, width, CAST_COLS):
                c1 = min(c0 + CAST_COLS, width)
                block = w2[c0:c1, :].T if n in transposed else w2[:, c0:c1]
                w_scr[:, base + c0:base + c1] = block.astype(w_scr.dtype)
            base += width


def _inproj_kernel(*refs, epilogues, has_rope, n_flog, n_weights, transposed):
    x_ref, g_ref = refs[:2]
    w_refs = refs[2:2 + n_weights]
    rest = list(refs[2 + n_weights:])
    pos_ref, inv_ref = (rest.pop(0), rest.pop(0)) if has_rope else (None, None)
    o_ref = rest.pop(0)
    flog_ref = rest.pop(0) if n_flog else None
    w_scr = rest.pop(0)
    _cast_weights(w_refs, w_scr, transposed)
    nb = len(epilogues)
    sub = ROW_TILE // ROW_GROUPS
    groups = [pl.ds(part * sub, sub) for part in range(ROW_GROUPS)]

    def pre_norm(rows):
        x = x_ref[rows, :]
        return (x * _rms_scale(x) * g_ref[...]).astype(_BF16)

    def finish(rows, j, a, rope):
        kind, scale = epilogues[j]
        if kind == "rot":
            a = a * rope[0] + pltpu.roll(a, RET_HEAD_DIM // 2, 1) * rope[1]
        if scale != 1.0:
            a = a * scale
        o_ref[j, rows, :] = a.astype(o_ref.dtype)

    hs = [pre_norm(rows) for rows in groups]
    ropes = [_rope_tables(pos_ref, inv_ref, part) if has_rope else None for part in range(ROW_GROUPS)]
    for rows, h, rope in zip(groups, hs, ropes):
        for jb in range(nb // 2):
            acc = jnp.dot(h, w_scr[:, jb * MXU_COLS:(jb + 1) * MXU_COLS], preferred_element_type=_F32)
            finish(rows, 2 * jb, acc[:, :LANES], rope)
            finish(rows, 2 * jb + 1, acc[:, LANES:], rope)
        if n_flog:
            base = nb * LANES
            flog_ref[rows, :] = jnp.dot(h, w_scr[:, base:base + LANES], preferred_element_type=_F32)


def _inproj(x2, g, weights, epilogues, rope=None, n_flog=0, transposed=()):
    t = x2.shape[0]
    nb = len(epilogues)
    ncols = sum(spec.block_shape[-2 if n in transposed else -1] for n, (_, spec) in enumerate(weights))
    assert ncols == (nb + n_flog) * LANES
    in_specs = [pl.BlockSpec((ROW_TILE, D_MODEL), lambda i: (i, 0)), _const_spec((1, D_MODEL))]
    in_specs += [spec for _, spec in weights]
    args = [x2, g.reshape(1, D_MODEL)] + [w for w, _ in weights]
    if rope is not None:
        in_specs += [pl.BlockSpec((ROW_TILE // 2, LANES), lambda i: (i, 0)), _const_spec((1, LANES))]
        args += list(rope)
    out_shape = [jax.ShapeDtypeStruct((nb, t, LANES), _BF16)]
    out_specs = [pl.BlockSpec((nb, ROW_TILE, LANES), lambda i: (0, i, 0))]
    if n_flog:
        out_shape.append(jax.ShapeDtypeStruct((t, LANES), _F32))
        out_specs.append(pl.BlockSpec((ROW_TILE, LANES), lambda i: (i, 0)))
    return pl.pallas_call(
        functools.partial(_inproj_kernel, epilogues=tuple(epilogues), has_rope=rope is not None, n_flog=n_flog,
                          n_weights=len(weights), transposed=tuple(transposed)),
        out_shape=out_shape,
        grid=(t // ROW_TILE,),
        in_specs=in_specs,
        out_specs=out_specs,
        scratch_shapes=[pltpu.VMEM((D_MODEL, ncols), _BF16)],
        compiler_params=pltpu.CompilerParams(dimension_semantics=("arbitrary",), vmem_limit_bytes=VMEM_LIMIT_BYTES),
        name="inproj_rope" if rope is not None else "inproj",
    )(*args)


def _layer_spec(shape, layer, rows=None):
    return pl.BlockSpec((1, shape[1] if rows is None else rows, shape[2]), lambda *_: (layer, 0, 0),
                        pipeline_mode=pl.Buffered(1))


def _retention_kernel(q_ref, k_ref, v_ref, gate_ref, dmask_ref, qd_ref, kd_ref, cd_ref, o_ref, *, n_chunks):
    c = RET_CHUNK
    dmask = dmask_ref[0]
    qd = qd_ref[0]
    kd = kd_ref[0]
    cd = cd_ref[0]

    inner, kv = [], []
    for n in range(n_chunks):
        rows = pl.ds(n * c, c)
        qc, kc, vc = q_ref[0, rows, :], k_ref[0, rows, :], v_ref[0, rows, :]
        inner.append((lax.dot_general(qc, kc, _NT, preferred_element_type=_F32) * dmask).astype(_BF16))
        k_scaled_t = (kc.astype(_F32) * kd).T.astype(_BF16)
        kv.append(jnp.dot(k_scaled_t, vc, preferred_element_type=_F32))

    state = jnp.zeros((RET_HEAD_DIM, RET_HEAD_DIM), _F32)
    for n in range(n_chunks):
        rows = pl.ds(n * c, c)
        q_scaled = (q_ref[0, rows, :].astype(_F32) * qd).astype(_BF16)
        lhs = jnp.concatenate([inner[n], q_scaled], axis=1)
        rhs = jnp.concatenate([v_ref[0, rows, :], state.astype(_BF16)], axis=0)
        out = jnp.dot(lhs, rhs, preferred_element_type=_F32)
        state = state * cd + kv[n]
        r = out * _rms_scale(out)
        g = gate_ref[0, rows, :].astype(_F32)
        o_ref[0, rows, :] = (r * (g * jax.nn.sigmoid(g))).astype(o_ref.dtype)


def _retention_tables():
    c = RET_CHUNK
    log_gamma = jnp.log(1.0 - 2.0 ** (-5.0 - jnp.arange(RET_HEADS, dtype=_F32)))
    idx = jnp.arange(c, dtype=_F32)
    rel = idx[:, None] - idx[None, :]
    dmask = jnp.where(rel[None] >= 0, jnp.exp(jnp.maximum(rel, 0.0)[None] * log_gamma[:, None, None]), 0.0)
    q_decay = jnp.exp((idx[None, :] + 1.0) * log_gamma[:, None])
    k_decay = jnp.exp((c - 1.0 - idx[None, :]) * log_gamma[:, None])
    chunk_decay = jnp.exp(c * log_gamma)
    full = (RET_HEADS, c, RET_HEAD_DIM)
    return (dmask,
            jnp.broadcast_to(q_decay[:, :, None], full),
            jnp.broadcast_to(k_decay[:, :, None], full),
            jnp.broadcast_to(chunk_decay[:, None, None], full))


def _retention(proj, batch, seq):
    nh = RET_HEADS
    p4 = proj.reshape(proj.shape[0], batch, seq, LANES)

    def slab(group):
        return pl.BlockSpec((nh, 1, seq, LANES), lambda b: (group, b, 0, 0))

    def table():
        return _const_spec((nh, RET_CHUNK, RET_HEAD_DIM))

    def body(q_ref, k_ref, v_ref, gate_ref, dm, qd, kd, cd, o_ref):
        for h in range(nh):
            one = pl.ds(h, 1)
            _retention_kernel(q_ref.at[h], k_ref.at[h], v_ref.at[h], gate_ref.at[h],
                              dm.at[one], qd.at[one], kd.at[one], cd.at[one], o_ref.at[h], n_chunks=seq // RET_CHUNK)

    out = pl.pallas_call(
        body,
        out_shape=jax.ShapeDtypeStruct((nh, batch, seq, LANES), _BF16),
        grid=(batch,),
        in_specs=[slab(0), slab(1), slab(2), slab(3), table(), table(), table(), table()],
        out_specs=pl.BlockSpec((nh, 1, seq, LANES), lambda b: (0, b, 0, 0)),
        compiler_params=pltpu.CompilerParams(dimension_semantics=("arbitrary",), vmem_limit_bytes=VMEM_LIMIT_BYTES),
        name="retention",
    )(p4, p4, p4, p4, *_retention_tables())
    return out.reshape(nh, batch * seq, LANES)


def _split3(x):
    hi = x.astype(_BF16)
    r1 = x - hi.astype(_F32)
    mid = r1.astype(_BF16)
    lo = (r1 - mid.astype(_F32)).astype(_BF16)
    return hi, mid, lo


C_PART_STRIDE = 16
ONES_LANE = LANES - 1


def _fox_kernel(q_ref, k_ref, v_ref, flog_ref, bias_ref, sel_ref, tri_ref, o_ref,
                cpack_ref, qa_ref, ka_ref, va_ref, *, seq, new_batch):
    lane = lax.broadcasted_iota(jnp.int32, (1, LANES), 1)
    low_half = lane < FOX_HEAD_DIM
    keep = (low_half, jnp.logical_not(low_half))
    sum_lane = (FOX_HEAD_DIM, 0)

    def build_cpack():
        tri = tri_ref[...]
        pad_rows = LANES - 3 * C_PART_STRIDE
        filler = jnp.where(lax.broadcasted_iota(jnp.int32, (pad_rows, CUM_TILE), 0) == pad_rows - 1, 1.0, 0.0)
        carry = jnp.zeros((C_PART_STRIDE, 1), _F32)
        for i in range(seq // CUM_TILE):
            rows = pl.ds(i * CUM_TILE, CUM_TILE)
            z = flog_ref[0, rows, :].T[:C_PART_STRIDE, :] + bias_ref[...]
            log_f = jnp.minimum(z, 0.0) - jnp.log1p(jnp.exp(-jnp.abs(z)))
            hi, mid, lo = _split3(log_f)
            cs = (jnp.dot(hi, tri, preferred_element_type=_F32)
                  + jnp.dot(mid, tri, preferred_element_type=_F32)
                  + jnp.dot(lo, tri, preferred_element_type=_F32)) + carry
            carry = cs[:, CUM_TILE - 1:CUM_TILE]
            parts = [part.astype(_F32) for part in _split3(cs * LOG2E)]
            packed_t = jnp.concatenate(parts + [filler], axis=0)
            cpack_ref[rows, :] = packed_t.T.astype(_BF16)

    if new_batch is not None:
        pl.when(new_batch)(build_cpack)

    blk = 512
    for r in range(seq // blk):
        rows = pl.ds(r * blk, blk)
        extras = jnp.dot(cpack_ref[rows, :], sel_ref[0], preferred_element_type=_F32).astype(_BF16)
        for e in range(2):
            qa_ref[e, rows, :] = jnp.where(keep[e], q_ref[0, rows, :], extras[:, :LANES])
            ka_ref[e, rows, :] = jnp.where(keep[e], k_ref[0, rows, :], extras[:, LANES:])
            ones_col = jnp.where(lane == sum_lane[e], 1.0, 0.0).astype(_BF16)
            va_ref[e, rows, :] = jnp.where(keep[e], v_ref[0, rows, :], ones_col)

    t = FOX_TILE
    row_id = lax.broadcasted_iota(jnp.int32, (t, t), 0)
    col_id = lax.broadcasted_iota(jnp.int32, (t, t), 1)
    causal = col_id <= row_id

    for qi in range(seq // t):
        q_rows = pl.ds(qi * t, t)
        acc = []
        for e in range(2):
            qa = qa_ref[e, q_rows, :]
            m_run = jnp.full((t, 1), NEG_BIG, _F32)
            a = jnp.zeros((t, LANES), _F32)
            for kj in range(qi + 1):
                k_rows = pl.ds(kj * t, t)
                s = lax.dot_general(qa, ka_ref[e, k_rows, :], _NT, preferred_element_type=_F32)
                if kj == qi:
                    s = jnp.where(causal, s, NEG_BIG)
                m_new = jnp.maximum(m_run, jnp.max(s, axis=-1, keepdims=True))
                p = jnp.exp2(s - m_new).astype(_BF16)
                a = a * jnp.exp2(m_run - m_new) + jnp.dot(p, va_ref[e, k_rows, :], preferred_element_type=_F32)
                m_run = m_new
            acc.append(a / a[:, sum_lane[e]:sum_lane[e] + 1])
        o_ref[0, q_rows, :] = jnp.where(low_half, acc[0], acc[1]).astype(o_ref.dtype)


def _fox_selectors():
    import numpy as np
    npair = FOX_HEADS // 2
    sel = np.zeros((npair, LANES, 2 * LANES), np.float32)
    for h in range(FOX_HEADS):
        p, e = divmod(h, 2)
        base = FOX_HEAD_DIM if e == 0 else 0
        qcol, kcol = base, LANES + base
        for part in range(3):
            sel[p, part * C_PART_STRIDE + h, qcol + part] = 1.0
            sel[p, ONES_LANE, kcol + part] = 1.0
            sel[p, ONES_LANE, qcol + 3 + part] = 1.0
            sel[p, part * C_PART_STRIDE + h, kcol + 3 + part] = -1.0
    return jnp.asarray(sel, _BF16)


def _fox(proj, flog, b_forget, batch, seq):
    npair = FOX_HEADS // 2
    p4 = proj.reshape(proj.shape[0], batch, seq, LANES)
    flog3 = flog.reshape(batch, seq, LANES)
    bias = jnp.zeros((C_PART_STRIDE,), _F32).at[:FOX_HEADS].set(b_forget.astype(_F32))
    bias = jnp.broadcast_to(bias[:, None], (C_PART_STRIDE, CUM_TILE))
    tri = jnp.triu(jnp.ones((CUM_TILE, CUM_TILE), _F32)).astype(_BF16)

    pps = FOX_PAIRS_PER_STEP

    def slab(offset):
        return pl.BlockSpec((pps, 1, seq, LANES), lambda b, p: (offset // pps + p, b, 0, 0))

    def body(q_ref, k_ref, v_ref, flog_ref, bias_ref, sel_ref, tri_ref, o_ref, *scratch):
        for n in range(pps):
            _fox_kernel(q_ref.at[n], k_ref.at[n], v_ref.at[n], flog_ref, bias_ref, sel_ref.at[pl.ds(n, 1)], tri_ref,
                        o_ref.at[n], *scratch, seq=seq, new_batch=(pl.program_id(1) == 0) if n == 0 else None)

    out = pl.pallas_call(
        body,
        out_shape=jax.ShapeDtypeStruct((npair, batch, seq, LANES), _BF16),
        grid=(batch, npair // pps),
        in_specs=[slab(0), slab(npair), slab(2 * npair),
                  pl.BlockSpec((1, seq, LANES), lambda b, p: (b, 0, 0)),
                  _const_spec((C_PART_STRIDE, CUM_TILE)),
                  pl.BlockSpec((pps, LANES, 2 * LANES), lambda b, p: (p, 0, 0)),
                  _const_spec((CUM_TILE, CUM_TILE))],
        out_specs=pl.BlockSpec((pps, 1, seq, LANES), lambda b, p: (p, b, 0, 0)),
        scratch_shapes=[pltpu.VMEM((seq, LANES), _BF16),
                        pltpu.VMEM((2, seq, LANES), _BF16),
                        pltpu.VMEM((2, seq, LANES), _BF16),
                        pltpu.VMEM((2, seq, LANES), _BF16)],
        compiler_params=pltpu.CompilerParams(dimension_semantics=("arbitrary", "arbitrary"),
                                             vmem_limit_bytes=VMEM_LIMIT_BYTES),
        name="fox_attention",
    )(p4, p4, p4, flog3, bias, _fox_selectors(), tri)
    return out.reshape(npair, batch * seq, LANES)


def _memattn_kernel(mem_ref, g_ref, w_ref, mq_ref, o_ref, w_scr, *, seq):
    _cast_weights([w_ref], w_scr)
    mem = mem_ref[0]
    kvn = (mem * _rms_scale(mem) * g_ref[...]).astype(_BF16)
    kv = jnp.dot(kvn, w_scr[...], preferred_element_type=_F32)
    lane = lax.broadcasted_iota(jnp.int32, (1, LANES), 1)
    low_half = lane < MEM_HEAD_DIM
    scale = MEM_HEAD_DIM ** -0.5
    for p in range(N_MEM_BLOCKS):
        mk = (kv[:, p * LANES:(p + 1) * LANES] * scale).astype(_BF16)
        mv = kv[:, MEM_WIDTH + p * LANES:MEM_WIDTH + (p + 1) * LANES].astype(_BF16)
        heads = []
        for e in range(2):
            keep = low_half if e == 0 else jnp.logical_not(low_half)
            sum_lane = MEM_HEAD_DIM if e == 0 else 0
            ones_col = jnp.where(lane == sum_lane, 1.0, 0.0).astype(_BF16)
            heads.append((jnp.where(keep, mk, jnp.zeros_like(mk)), jnp.where(keep, mv, ones_col), sum_lane))
        row_tiles = [pl.ds(i * MEM_ROWS, MEM_ROWS) for i in range(seq // MEM_ROWS)]
        scores = [[lax.dot_general(mq_ref[p, rows, :], mk_e, _NT, preferred_element_type=_F32)
                   for mk_e, _, _ in heads] for rows in row_tiles]
        probs = [[jnp.exp(s - jnp.max(s, axis=-1, keepdims=True)).astype(_BF16) for s in pair] for pair in scores]
        for rows, pair in zip(row_tiles, probs):
            outs = []
            for prob, (_, mv_e, sum_lane) in zip(pair, heads):
                out = jnp.dot(prob, mv_e, preferred_element_type=_F32)
                outs.append(out / out[:, sum_lane:sum_lane + 1])
            o_ref[p, rows, :] = jnp.where(low_half, outs[0], outs[1]).astype(o_ref.dtype)


def _memattn(proj, mq_offset, mem, g_mem, w_mem_kv, layer, batch, seq):
    p4 = proj.reshape(proj.shape[0], batch, seq, LANES)
    mem_len = mem.shape[1]

    def body(mem_ref, g_ref, w_ref, mq_ref, o_ref, w_scr):
        _memattn_kernel(mem_ref, g_ref, w_ref, mq_ref.at[:, 0], o_ref.at[:, 0], w_scr, seq=seq)

    out = pl.pallas_call(
        body,
        out_shape=jax.ShapeDtypeStruct((N_MEM_BLOCKS, batch, seq, LANES), _BF16),
        grid=(batch,),
        in_specs=[pl.BlockSpec((1, mem_len, D_MODEL), lambda b: (b, 0, 0)),
                  _const_spec((1, D_MODEL)),
                  _layer_spec(w_mem_kv.shape, layer),
                  pl.BlockSpec((N_MEM_BLOCKS, 1, seq, LANES), lambda b: (mq_offset // N_MEM_BLOCKS, b, 0, 0))],
        out_specs=pl.BlockSpec((N_MEM_BLOCKS, 1, seq, LANES), lambda b: (0, b, 0, 0)),
        scratch_shapes=[pltpu.VMEM((D_MODEL, 2 * MEM_WIDTH), _BF16)],
        compiler_params=pltpu.CompilerParams(dimension_semantics=("arbitrary",), vmem_limit_bytes=VMEM_LIMIT_BYTES),
        name="memory_attention",
    )(mem, g_mem.reshape(1, D_MODEL), w_mem_kv, p4)
    return out.reshape(N_MEM_BLOCKS, batch * seq, LANES)


def _tail_kernel(x_ref, so_ref, mo_ref, wo_ref, g1_ref, g2_ref, g3_ref, wup_ref, wdn_ref, o_ref):
    sub = ROW_TILE // ROW_GROUPS
    groups = [pl.ds(part * sub, sub) for part in range(ROW_GROUPS)]

    def out_proj(rows):
        mixed = jnp.concatenate([so_ref[j, rows, :] for j in range(N_SELF_BLOCKS)]
                                + [mo_ref[j, rows, :] for j in range(N_MEM_BLOCKS)], axis=1)
        return jnp.dot(mixed, wo_ref[0], preferred_element_type=_F32)

    def mid_norms(rows, y):
        x1 = x_ref[rows, :] + y * _rms_scale(y) * g1_ref[...]
        return x1, (x1 * _rms_scale(x1) * g2_ref[...]).astype(_BF16)

    def mlp(h):
        acc = jnp.zeros((sub, D_MODEL), _F32)
        for c in range(D_FF // FF_CHUNK):
            cols = slice(c * FF_CHUNK, (c + 1) * FF_CHUNK)
            u = jnp.maximum(jnp.dot(h, wup_ref[0, :, cols], preferred_element_type=_F32), 0.0)
            acc = acc + jnp.dot((u * u).astype(_BF16), wdn_ref[0, cols, :], preferred_element_type=_F32)
        return acc

    ys = [out_proj(rows) for rows in groups]
    mids = [mid_norms(rows, y) for rows, y in zip(groups, ys)]
    for rows, (x1, h) in zip(groups, mids):
        acc = mlp(h)
        o_ref[rows, :] = x1 + acc * _rms_scale(acc) * g3_ref[...]


def _tail(x2, self_out, mem_out, layer, w_out, g_post_mix, g_pre_mlp, g_post_mlp, w_up, w_down):
    t = x2.shape[0]
    row = lambda: pl.BlockSpec((ROW_TILE, D_MODEL), lambda i: (i, 0))
    gain = lambda g: g.reshape(1, D_MODEL)
    return pl.pallas_call(
        _tail_kernel,
        out_shape=jax.ShapeDtypeStruct((t, D_MODEL), _F32),
        grid=(t // ROW_TILE,),
        in_specs=[row(),
                  pl.BlockSpec((N_SELF_BLOCKS, ROW_TILE, LANES), lambda i: (0, i, 0)),
                  pl.BlockSpec((N_MEM_BLOCKS, ROW_TILE, LANES), lambda i: (0, i, 0)),
                  _layer_spec(w_out.shape, layer),
                  _const_spec((1, D_MODEL)), _const_spec((1, D_MODEL)), _const_spec((1, D_MODEL)),
                  _layer_spec(w_up.shape, layer), _layer_spec(w_down.shape, layer)],
        out_specs=row(),
        compiler_params=pltpu.CompilerParams(dimension_semantics=("arbitrary",), vmem_limit_bytes=VMEM_LIMIT_BYTES),
        name="outproj_mlp",
    )(x2, self_out, mem_out, w_out, gain(g_post_mix), gain(g_pre_mlp), gain(g_post_mlp), w_up, w_down)


def kernel(x, mem, positions, w_in_ret, w_in_fox, b_forget, w_mem_kv, w_out, w_up, w_down,
           g_pre_mix, g_post_mix, g_pre_mlp, g_post_mlp, g_mem):
    batch, seq, _ = x.shape
    depth = w_out.shape[0]
    assert ROW_TILE == 2 * ROPE_PACK * ROW_GROUPS and (batch * seq) % ROW_TILE == 0 and seq % FOX_TILE == 0 and seq % RET_CHUNK == 0
    x2 = x.reshape(batch * seq, D_MODEL)
    rope = _rope_inputs(positions)
    w_out_b, w_up_b, w_down_b = w_out.astype(_BF16), w_up.astype(_BF16), w_down.astype(_BF16)
    for i in range(depth):
        j = i // 2
        if i % 2 == 0:
            epi = ([("rot", 1.0)] * RET_HEADS + [("rot", RET_HEAD_DIM ** -0.5)] * RET_HEADS
                   + [("plain", 1.0)] * (2 * RET_HEADS + N_MEM_BLOCKS))
            (proj,) = _inproj(x2, g_pre_mix[i], [(w_in_ret, _layer_spec(w_in_ret.shape, j))], epi, rope=rope)
            self_out = _retention(proj, batch, seq)
            mq_offset = 4 * RET_HEADS
        else:
            qkv = 3 * FOX_WIDTH
            w_t = jnp.swapaxes(w_in_fox, 1, 2)
            rest_t = w_t[j, qkv:, :]
            w_rest_t = jnp.concatenate([rest_t[FOX_HEADS:],
                                        jnp.pad(rest_t[:FOX_HEADS], ((0, LANES - FOX_HEADS), (0, 0)))], axis=0)
            npair = FOX_HEADS // 2
            epi = ([("plain", FOX_HEAD_DIM ** -0.5 * LOG2E)] * npair + [("plain", 1.0)] * (2 * npair + N_MEM_BLOCKS))
            weights = [(w_t, _layer_spec(w_t.shape, j, rows=qkv)), (w_rest_t, _const_spec(w_rest_t.shape))]
            proj, flog = _inproj(x2, g_pre_mix[i], weights, epi, n_flog=1, transposed=(0, 1))
            self_out = _fox(proj, flog, b_forget[j], batch, seq)
            mq_offset = 3 * npair
        mem_out = _memattn(proj, mq_offset, mem, g_mem[i], w_mem_kv, i, batch, seq)
        x2 = _tail(x2, self_out, mem_out, i, w_out_b, g_post_mix[i], g_pre_mlp[i], g_post_mlp[i], w_up_b, w_down_b)
    return x2.reshape(batch, seq, D_MODEL)
```

```python
import functools

import jax
import jax.numpy as jnp
from jax import lax
from jax.experimental import pallas as pl
from jax.experimental.pallas import tpu as pltpu

D_MODEL = 1024
MEM_HEADS = 4
MEM_HEAD_DIM = 64
MEM_WIDTH = MEM_HEADS * MEM_HEAD_DIM
RET_HEAD_DIM = 128
RET_HEADS = 6
RET_WIDTH = RET_HEADS * RET_HEAD_DIM
RET_CHUNK = 128
ROPE_BASE = 10000.0
FOX_HEAD_DIM = 64
FOX_HEADS = 12
FOX_WIDTH = FOX_HEADS * FOX_HEAD_DIM
D_FF = 4 * D_MODEL
EPS = 1e-6

LANES = 128
MXU_COLS = 256
VMEM_LIMIT_BYTES = 56 * 1024 * 1024
LOG2E = 1.4426950408889634
NEG_BIG = -1e30

ROW_TILE = 1024
ROW_GROUPS = 2
FF_CHUNK = 512
FOX_TILE = 256
CUM_TILE = 256
MEM_ROWS = 256

N_SELF_BLOCKS = RET_WIDTH // LANES
N_MEM_BLOCKS = MEM_WIDTH // LANES

_NT = (((1,), (1,)), ((), ()))
_F32 = jnp.float32
_BF16 = jnp.bfloat16


def _rms_scale(x):
    return lax.rsqrt(jnp.mean(x * x, axis=-1, keepdims=True) + EPS)


def _const_spec(shape):
    nd = len(shape)
    return pl.BlockSpec(shape, lambda *_: (0,) * nd, pipeline_mode=pl.Buffered(1))


ROPE_PACK = ROW_TILE // ROW_GROUPS // 2


def _rope_tables(pos_ref, inv_ref, part):
    ang = pos_ref[pl.ds(part * ROPE_PACK, ROPE_PACK), :] * inv_ref[...]
    c, s = jnp.cos(ang), jnp.sin(ang)
    half = RET_HEAD_DIM // 2
    c_sw, s_sw = pltpu.roll(c, half, 1), pltpu.roll(s, half, 1)
    low = lax.broadcasted_iota(jnp.int32, (1, LANES), 1) < half
    return (jnp.concatenate([jnp.where(low, c, c_sw), jnp.where(low, c_sw, c)], axis=0),
            jnp.concatenate([jnp.where(low, -s, s_sw), jnp.where(low, -s_sw, s)], axis=0))


def _rope_inputs(positions):
    t = positions.size
    half = RET_HEAD_DIM // 2
    inv_freq = ROPE_BASE ** (-jnp.arange(0, RET_HEAD_DIM, 2, dtype=_F32) / RET_HEAD_DIM)
    pos2 = positions.reshape(t // (2 * ROPE_PACK), 2, ROPE_PACK).transpose(0, 2, 1).reshape(t // 2, 2)
    return jnp.repeat(pos2.astype(_F32), half, axis=1), jnp.tile(inv_freq, 2).reshape(1, LANES)


CAST_COLS = 512


def _cast_weights(w_refs, w_scr, transposed=()):
    @pl.when(pl.program_id(0) == 0)
    def _():
        base = 0
        for n, w_ref in enumerate(w_refs):
            w2 = w_ref.at[0] if len(w_ref.shape) == 3 else w_ref
            width = w2.shape[0] if n in transposed else w2.shape[1]
            for c0 in range(0, width, CAST_COLS):
                c1 = min(c0 + CAST_COLS, width)
                block = w2[c0:c1, :].T if n in transposed else w2[:, c0:c1]
                w_scr[:, base + c0:base + c1] = block.astype(w_scr.dtype)
            base += width


def _inproj_kernel(*refs, epilogues, has_rope, n_flog, n_weights, transposed):
    x_ref, g_ref = refs[:2]
    w_refs = refs[2:2 + n_weights]
    rest = list(refs[2 + n_weights:])
    pos_ref, inv_ref = (rest.pop(0), rest.pop(0)) if has_rope else (None, None)
    o_ref = rest.pop(0)
    flog_ref = rest.pop(0) if n_flog else None
    w_scr = rest.pop(0)
    _cast_weights(w_refs, w_scr, transposed)
    nb = len(epilogues)
    sub = ROW_TILE // ROW_GROUPS
    groups = [pl.ds(part * sub, sub) for part in range(ROW_GROUPS)]

    def pre_norm(rows):
        x = x_ref[rows, :]
        return (x * _rms_scale(x) * g_ref[...]).astype(_BF16)

    def finish(rows, j, a, rope):
        kind, scale = epilogues[j]
        if kind == "rot":
            a = a * rope[0] + pltpu.roll(a, RET_HEAD_DIM // 2, 1) * rope[1]
        if scale != 1.0:
            a = a * scale
        o_ref[j, rows, :] = a.astype(o_ref.dtype)

    hs = [pre_norm(rows) for rows in groups]
    ropes = [_rope_tables(pos_ref, inv_ref, part) if has_rope else None for part in range(ROW_GROUPS)]
    for rows, h, rope in zip(groups, hs, ropes):
        for jb in range(nb // 2):
            acc = jnp.dot(h, w_scr[:, jb * MXU_COLS:(jb + 1) * MXU_COLS], preferred_element_type=_F32)
            finish(rows, 2 * jb, acc[:, :LANES], rope)
            finish(rows, 2 * jb + 1, acc[:, LANES:], rope)
        if n_flog:
            base = nb * LANES
            flog_ref[rows, :] = jnp.dot(h, w_scr[:, base:base + LANES], preferred_element_type=_F32)


def _inproj(x2, g, weights, epilogues, rope=None, n_flog=0, transposed=()):
    t = x2.shape[0]
    nb = len(epilogues)
    ncols = sum(spec.block_shape[-2 if n in transposed else -1] for n, (_, spec) in enumerate(weights))
    assert ncols == (nb + n_flog) * LANES
    in_specs = [pl.BlockSpec((ROW_TILE, D_MODEL), lambda i: (i, 0)), _const_spec((1, D_MODEL))]
    in_specs += [spec for _, spec in weights]
    args = [x2, g.reshape(1, D_MODEL)] + [w for w, _ in weights]
    if rope is not None:
        in_specs += [pl.BlockSpec((ROW_TILE // 2, LANES), lambda i: (i, 0)), _const_spec((1, LANES))]
        args += list(rope)
    out_shape = [jax.ShapeDtypeStruct((nb, t, LANES), _BF16)]
    out_specs = [pl.BlockSpec((nb, ROW_TILE, LANES), lambda i: (0, i, 0))]
    if n_flog:
        out_shape.append(jax.ShapeDtypeStruct((t, LANES), _F32))
        out_specs.append(pl.BlockSpec((ROW_TILE, LANES), lambda i: (i, 0)))
    return pl.pallas_call(
        functools.partial(_inproj_kernel, epilogues=tuple(epilogues), has_rope=rope is not None, n_flog=n_flog,
                          n_weights=len(weights), transposed=tuple(transposed)),
        out_shape=out_shape,
        grid=(t // ROW_TILE,),
        in_specs=in_specs,
        out_specs=out_specs,
        scratch_shapes=[pltpu.VMEM((D_MODEL, ncols), _BF16)],
        compiler_params=pltpu.CompilerParams(dimension_semantics=("arbitrary",), vmem_limit_bytes=VMEM_LIMIT_BYTES),
        name="inproj_rope" if rope is not None else "inproj",
    )(*args)


def _layer_spec(shape, layer, rows=None):
    return pl.BlockSpec((1, shape[1] if rows is None else rows, shape[2]), lambda *_: (layer, 0, 0),
                        pipeline_mode=pl.Buffered(1))


def _retention_kernel(q_ref, k_ref, v_ref, gate_ref, dmask_ref, qd_ref, kd_ref, cd_ref, o_ref, *, n_chunks):
    c = RET_CHUNK
    dmask = dmask_ref[0]
    qd = qd_ref[0]
    kd = kd_ref[0]
    cd = cd_ref[0]

    inner, kv = [], []
    for n in range(n_chunks):
        rows = pl.ds(n * c, c)
        qc, kc, vc = q_ref[0, rows, :], k_ref[0, rows, :], v_ref[0, rows, :]
        inner.append((lax.dot_general(qc, kc, _NT, preferred_element_type=_F32) * dmask).astype(_BF16))
        k_scaled_t = (kc.astype(_F32) * kd).T.astype(_BF16)
        kv.append(jnp.dot(k_scaled_t, vc, preferred_element_type=_F32))

    state = jnp.zeros((RET_HEAD_DIM, RET_HEAD_DIM), _F32)
    for n in range(n_chunks):
        rows = pl.ds(n * c, c)
        q_scaled = (q_ref[0, rows, :].astype(_F32) * qd).astype(_BF16)
        lhs = jnp.concatenate([inner[n], q_scaled], axis=1)
        rhs = jnp.concatenate([v_ref[0, rows, :], state.astype(_BF16)], axis=0)
        out = jnp.dot(lhs, rhs, preferred_element_type=_F32)
        state = state * cd + kv[n]
        r = out * _rms_scale(out)
        g = gate_ref[0, rows, :].astype(_F32)
        o_ref[0, rows, :] = (r * (g * jax.nn.sigmoid(g))).astype(o_ref.dtype)


def _retention_tables():
    c = RET_CHUNK
    log_gamma = jnp.log(1.0 - 2.0 ** (-5.0 - jnp.arange(RET_HEADS, dtype=_F32)))
    idx = jnp.arange(c, dtype=_F32)
    rel = idx[:, None] - idx[None, :]
    dmask = jnp.where(rel[None] >= 0, jnp.exp(jnp.maximum(rel, 0.0)[None] * log_gamma[:, None, None]), 0.0)
    q_decay = jnp.exp((idx[None, :] + 1.0) * log_gamma[:, None])
    k_decay = jnp.exp((c - 1.0 - idx[None, :]) * log_gamma[:, None])
    chunk_decay = jnp.exp(c * log_gamma)
    full = (RET_HEADS, c, RET_HEAD_DIM)
    return (dmask,
            jnp.broadcast_to(q_decay[:, :, None], full),
            jnp.broadcast_to(k_decay[:, :, None], full),
            jnp.broadcast_to(chunk_decay[:, None, None], full))


def _retention(proj, batch, seq):
    nh = RET_HEADS
    p4 = proj.reshape(proj.shape[0], batch, seq, LANES)

    def slab(group):
        return pl.BlockSpec((nh, 1, seq, LANES), lambda b: (group, b, 0, 0))

    def table():
        return _const_spec((nh, RET_CHUNK, RET_HEAD_DIM))

    def body(q_ref, k_ref, v_ref, gate_ref, dm, qd, kd, cd, o_ref):
        for h in range(nh):
            one = pl.ds(h, 1)
            _retention_kernel(q_ref.at[h], k_ref.at[h], v_ref.at[h], gate_ref.at[h],
                              dm.at[one], qd.at[one], kd.at[one], cd.at[one], o_ref.at[h], n_chunks=seq // RET_CHUNK)

    out = pl.pallas_call(
        body,
        out_shape=jax.ShapeDtypeStruct((nh, batch, seq, LANES), _BF16),
        grid=(batch,),
        in_specs=[slab(0), slab(1), slab(2), slab(3), table(), table(), table(), table()],
        out_specs=pl.BlockSpec((nh, 1, seq, LANES), lambda b: (0, b, 0, 0)),
        compiler_params=pltpu.CompilerParams(dimension_semantics=("arbitrary",), vmem_limit_bytes=VMEM_LIMIT_BYTES),
        name="retention",
    )(p4, p4, p4, p4, *_retention_tables())
    return out.reshape(nh, batch * seq, LANES)


def _split3(x):
    hi = x.astype(_BF16)
    r1 = x - hi.astype(_F32)
    mid = r1.astype(_BF16)
    lo = (r1 - mid.astype(_F32)).astype(_BF16)
    return hi, mid, lo


C_PART_STRIDE = 16
ONES_LANE = LANES - 1


def _fox_kernel(q_ref, k_ref, v_ref, flog_ref, bias_ref, sel_ref, tri_ref, o_ref,
                cpack_ref, qa_ref, ka_ref, va_ref, *, seq, new_batch):
    lane = lax.broadcasted_iota(jnp.int32, (1, LANES), 1)
    low_half = lane < FOX_HEAD_DIM
    keep = (low_half, jnp.logical_not(low_half))
    sum_lane = (FOX_HEAD_DIM, 0)

    def build_cpack():
        tri = tri_ref[...]
        pad_rows = LANES - 3 * C_PART_STRIDE
        filler = jnp.where(lax.broadcasted_iota(jnp.int32, (pad_rows, CUM_TILE), 0) == pad_rows - 1, 1.0, 0.0)
        carry = jnp.zeros((C_PART_STRIDE, 1), _F32)
        for i in range(seq // CUM_TILE):
            rows = pl.ds(i * CUM_TILE, CUM_TILE)
            z = flog_ref[0, rows, :].T[:C_PART_STRIDE, :] + bias_ref[...]
            log_f = jnp.minimum(z, 0.0) - jnp.log1p(jnp.exp(-jnp.abs(z)))
            hi, mid, lo = _split3(log_f)
            cs = (jnp.dot(hi, tri, preferred_element_type=_F32)
                  + jnp.dot(mid, tri, preferred_element_type=_F32)
                  + jnp.dot(lo, tri, preferred_element_type=_F32)) + carry
            carry = cs[:, CUM_TILE - 1:CUM_TILE]
            parts = [part.astype(_F32) for part in _split3(cs * LOG2E)]
            packed_t = jnp.concatenate(parts + [filler], axis=0)
            cpack_ref[rows, :] = packed_t.T.astype(_BF16)

    pl.when(new_batch)(build_cpack)

    blk = 512
    for r in range(seq // blk):
        rows = pl.ds(r * blk, blk)
        extras = jnp.dot(cpack_ref[rows, :], sel_ref[0], preferred_element_type=_F32).astype(_BF16)
        for e in range(2):
            qa_ref[e, rows, :] = jnp.where(keep[e], q_ref[0, rows, :], extras[:, :LANES])
            ka_ref[e, rows, :] = jnp.where(keep[e], k_ref[0, rows, :], extras[:, LANES:])
            ones_col = jnp.where(lane == sum_lane[e], 1.0, 0.0).astype(_BF16)
            va_ref[e, rows, :] = jnp.where(keep[e], v_ref[0, rows, :], ones_col)

    t = FOX_TILE
    row_id = lax.broadcasted_iota(jnp.int32, (t, t), 0)
    col_id = lax.broadcasted_iota(jnp.int32, (t, t), 1)
    causal = col_id <= row_id

    for qi in range(seq // t):
        q_rows = pl.ds(qi * t, t)
        acc = []
        for e in range(2):
            qa = qa_ref[e, q_rows, :]
            m_run = jnp.full((t, 1), NEG_BIG, _F32)
            a = jnp.zeros((t, LANES), _F32)
            for kj in range(qi + 1):
                k_rows = pl.ds(kj * t, t)
                s = lax.dot_general(qa, ka_ref[e, k_rows, :], _NT, preferred_element_type=_F32)
                if kj == qi:
                    s = jnp.where(causal, s, NEG_BIG)
                m_new = jnp.maximum(m_run, jnp.max(s, axis=-1, keepdims=True))
                p = jnp.exp2(s - m_new).astype(_BF16)
                a = a * jnp.exp2(m_run - m_new) + jnp.dot(p, va_ref[e, k_rows, :], preferred_element_type=_F32)
                m_run = m_new
            acc.append(a / a[:, sum_lane[e]:sum_lane[e] + 1])
        o_ref[0, q_rows, :] = jnp.where(low_half, acc[0], acc[1]).astype(o_ref.dtype)


def _fox_selectors():
    import numpy as np
    npair = FOX_HEADS // 2
    sel = np.zeros((npair, LANES, 2 * LANES), np.float32)
    for h in range(FOX_HEADS):
        p, e = divmod(h, 2)
        base = FOX_HEAD_DIM if e == 0 else 0
        qcol, kcol = base, LANES + base
        for part in range(3):
            sel[p, part * C_PART_STRIDE + h, qcol + part] = 1.0
            sel[p, ONES_LANE, kcol + part] = 1.0
            sel[p, ONES_LANE, qcol + 3 + part] = 1.0
            sel[p, part * C_PART_STRIDE + h, kcol + 3 + part] = -1.0
    return jnp.asarray(sel, _BF16)


def _fox(proj, flog, b_forget, batch, seq):
    npair = FOX_HEADS // 2
    p4 = proj.reshape(proj.shape[0], batch, seq, LANES)
    flog3 = flog.reshape(batch, seq, LANES)
    bias = jnp.zeros((C_PART_STRIDE,), _F32).at[:FOX_HEADS].set(b_forget.astype(_F32))
    bias = jnp.broadcast_to(bias[:, None], (C_PART_STRIDE, CUM_TILE))
    tri = jnp.triu(jnp.ones((CUM_TILE, CUM_TILE), _F32)).astype(_BF16)

    def slab(offset):
        return pl.BlockSpec((1, 1, seq, LANES), lambda b, p: (offset + p, b, 0, 0))

    def body(q_ref, k_ref, v_ref, flog_ref, bias_ref, sel_ref, tri_ref, o_ref, *scratch):
        _fox_kernel(q_ref.at[0], k_ref.at[0], v_ref.at[0], flog_ref, bias_ref, sel_ref, tri_ref,
                    o_ref.at[0], *scratch, seq=seq, new_batch=pl.program_id(1) == 0)

    out = pl.pallas_call(
        body,
        out_shape=jax.ShapeDtypeStruct((npair, batch, seq, LANES), _BF16),
        grid=(batch, npair),
        in_specs=[slab(0), slab(npair), slab(2 * npair),
                  pl.BlockSpec((1, seq, LANES), lambda b, p: (b, 0, 0)),
                  _const_spec((C_PART_STRIDE, CUM_TILE)),
                  pl.BlockSpec((1, LANES, 2 * LANES), lambda b, p: (p, 0, 0)),
                  _const_spec((CUM_TILE, CUM_TILE))],
        out_specs=pl.BlockSpec((1, 1, seq, LANES), lambda b, p: (p, b, 0, 0)),
        scratch_shapes=[pltpu.VMEM((seq, LANES), _BF16),
                        pltpu.VMEM((2, seq, LANES), _BF16),
                        pltpu.VMEM((2, seq, LANES), _BF16),
                        pltpu.VMEM((2, seq, LANES), _BF16)],
        compiler_params=pltpu.CompilerParams(dimension_semantics=("arbitrary", "arbitrary"),
                                             vmem_limit_bytes=VMEM_LIMIT_BYTES),
        name="fox_attention",
    )(p4, p4, p4, flog3, bias, _fox_selectors(), tri)
    return out.reshape(npair, batch * seq, LANES)


def _memattn_kernel(mem_ref, g_ref, w_ref, mq_ref, o_ref, w_scr, *, seq):
    _cast_weights([w_ref], w_scr)
    mem = mem_ref[0]
    kvn = (mem * _rms_scale(mem) * g_ref[...]).astype(_BF16)
    kv = jnp.dot(kvn, w_scr[...], preferred_element_type=_F32)
    lane = lax.broadcasted_iota(jnp.int32, (1, LANES), 1)
    low_half = lane < MEM_HEAD_DIM
    scale = MEM_HEAD_DIM ** -0.5
    for p in range(N_MEM_BLOCKS):
        mk = (kv[:, p * LANES:(p + 1) * LANES] * scale).astype(_BF16)
        mv = kv[:, MEM_WIDTH + p * LANES:MEM_WIDTH + (p + 1) * LANES].astype(_BF16)
        heads = []
        for e in range(2):
            keep = low_half if e == 0 else jnp.logical_not(low_half)
            sum_lane = MEM_HEAD_DIM if e == 0 else 0
            ones_col = jnp.where(lane == sum_lane, 1.0, 0.0).astype(_BF16)
            heads.append((jnp.where(keep, mk, jnp.zeros_like(mk)), jnp.where(keep, mv, ones_col), sum_lane))
        row_tiles = [pl.ds(i * MEM_ROWS, MEM_ROWS) for i in range(seq // MEM_ROWS)]
        scores = [[lax.dot_general(mq_ref[p, rows, :], mk_e, _NT, preferred_element_type=_F32)
                   for mk_e, _, _ in heads] for rows in row_tiles]
        probs = [[jnp.exp(s - jnp.max(s, axis=-1, keepdims=True)).astype(_BF16) for s in pair] for pair in scores]
        for rows, pair in zip(row_tiles, probs):
            outs = []
            for prob, (_, mv_e, sum_lane) in zip(pair, heads):
                out = jnp.dot(prob, mv_e, preferred_element_type=_F32)
                outs.append(out / out[:, sum_lane:sum_lane + 1])
            o_ref[p, rows, :] = jnp.where(low_half, outs[0], outs[1]).astype(o_ref.dtype)


def _memattn(proj, mq_offset, mem, g_mem, w_mem_kv, layer, batch, seq):
    p4 = proj.reshape(proj.shape[0], batch, seq, LANES)
    mem_len = mem.shape[1]

    def body(mem_ref, g_ref, w_ref, mq_ref, o_ref, w_scr):
        _memattn_kernel(mem_ref, g_ref, w_ref, mq_ref.at[:, 0], o_ref.at[:, 0], w_scr, seq=seq)

    out = pl.pallas_call(
        body,
        out_shape=jax.ShapeDtypeStruct((N_MEM_BLOCKS, batch, seq, LANES), _BF16),
        grid=(batch,),
        in_specs=[pl.BlockSpec((1, mem_len, D_MODEL), lambda b: (b, 0, 0)),
                  _const_spec((1, D_MODEL)),
                  _layer_spec(w_mem_kv.shape, layer),
                  pl.BlockSpec((N_MEM_BLOCKS, 1, seq, LANES), lambda b: (mq_offset // N_MEM_BLOCKS, b, 0, 0))],
        out_specs=pl.BlockSpec((N_MEM_BLOCKS, 1, seq, LANES), lambda b: (0, b, 0, 0)),
        scratch_shapes=[pltpu.VMEM((D_MODEL, 2 * MEM_WIDTH), _BF16)],
        compiler_params=pltpu.CompilerParams(dimension_semantics=("arbitrary",), vmem_limit_bytes=VMEM_LIMIT_BYTES),
        name="memory_attention",
    )(mem, g_mem.reshape(1, D_MODEL), w_mem_kv, p4)
    return out.reshape(N_MEM_BLOCKS, batch * seq, LANES)


def _tail_kernel(x_ref, so_ref, mo_ref, wo_ref, g1_ref, g2_ref, g3_ref, wup_ref, wdn_ref, o_ref):
    sub = ROW_TILE // ROW_GROUPS
    groups = [pl.ds(part * sub, sub) for part in range(ROW_GROUPS)]

    def out_proj(rows):
        mixed = jnp.concatenate([so_ref[j, rows, :] for j in range(N_SELF_BLOCKS)]
                                + [mo_ref[j, rows, :] for j in range(N_MEM_BLOCKS)], axis=1)
        return jnp.dot(mixed, wo_ref[0], preferred_element_type=_F32)

    def mid_norms(rows, y):
        x1 = x_ref[rows, :] + y * _rms_scale(y) * g1_ref[...]
        return x1, (x1 * _rms_scale(x1) * g2_ref[...]).astype(_BF16)

    def mlp(h):
        acc = jnp.zeros((sub, D_MODEL), _F32)
        for c in range(D_FF // FF_CHUNK):
            cols = slice(c * FF_CHUNK, (c + 1) * FF_CHUNK)
            u = jnp.maximum(jnp.dot(h, wup_ref[0, :, cols], preferred_element_type=_F32), 0.0)
            acc = acc + jnp.dot((u * u).astype(_BF16), wdn_ref[0, cols, :], preferred_element_type=_F32)
        return acc

    ys = [out_proj(rows) for rows in groups]
    mids = [mid_norms(rows, y) for rows, y in zip(groups, ys)]
    for rows, (x1, h) in zip(groups, mids):
        acc = mlp(h)
        o_ref[rows, :] = x1 + acc * _rms_scale(acc) * g3_ref[...]


def _tail(x2, self_out, mem_out, layer, w_out, g_post_mix, g_pre_mlp, g_post_mlp, w_up, w_down):
    t = x2.shape[0]
    row = lambda: pl.BlockSpec((ROW_TILE, D_MODEL), lambda i: (i, 0))
    gain = lambda g: g.reshape(1, D_MODEL)
    return pl.pallas_call(
        _tail_kernel,
        out_shape=jax.ShapeDtypeStruct((t, D_MODEL), _F32),
        grid=(t // ROW_TILE,),
        in_specs=[row(),
                  pl.BlockSpec((N_SELF_BLOCKS, ROW_TILE, LANES), lambda i: (0, i, 0)),
                  pl.BlockSpec((N_MEM_BLOCKS, ROW_TILE, LANES), lambda i: (0, i, 0)),
                  _layer_spec(w_out.shape, layer),
                  _const_spec((1, D_MODEL)), _const_spec((1, D_MODEL)), _const_spec((1, D_MODEL)),
                  _layer_spec(w_up.shape, layer), _layer_spec(w_down.shape, layer)],
        out_specs=row(),
        compiler_params=pltpu.CompilerParams(dimension_semantics=("arbitrary",), vmem_limit_bytes=VMEM_LIMIT_BYTES),
        name="outproj_mlp",
    )(x2, self_out, mem_out, w_out, gain(g_post_mix), gain(g_pre_mlp), gain(g_post_mlp), w_up, w_down)


def kernel(x, mem, positions, w_in_ret, w_in_fox, b_forget, w_mem_kv, w_out, w_up, w_down,
           g_pre_mix, g_post_mix, g_pre_mlp, g_post_mlp, g_mem):
    batch, seq, _ = x.shape
    depth = w_out.shape[0]
    assert (batch * seq) % ROW_TILE == 0 and seq % FOX_TILE == 0 and seq % RET_CHUNK == 0
    x2 = x.reshape(batch * seq, D_MODEL)
    rope = _rope_inputs(positions)
    w_out_b, w_up_b, w_down_b = w_out.astype(_BF16), w_up.astype(_BF16), w_down.astype(_BF16)
    for i in range(depth):
        j = i // 2
        if i % 2 == 0:
            epi = ([("rot", 1.0)] * RET_HEADS + [("rot", RET_HEAD_DIM ** -0.5)] * RET_HEADS
                   + [("plain", 1.0)] * (2 * RET_HEADS + N_MEM_BLOCKS))
            (proj,) = _inproj(x2, g_pre_mix[i], [(w_in_ret, _layer_spec(w_in_ret.shape, j))], epi, rope=rope)
            self_out = _retention(proj, batch, seq)
            mq_offset = 4 * RET_HEADS
        else:
            qkv = 3 * FOX_WIDTH
            w_t = jnp.swapaxes(w_in_fox, 1, 2)
            rest_t = w_t[j, qkv:, :]
            w_rest_t = jnp.concatenate([rest_t[FOX_HEADS:],
                                        jnp.pad(rest_t[:FOX_HEADS], ((0, LANES - FOX_HEADS), (0, 0)))], axis=0)
            npair = FOX_HEADS // 2
            epi = ([("plain", FOX_HEAD_DIM ** -0.5 * LOG2E)] * npair + [("plain", 1.0)] * (2 * npair + N_MEM_BLOCKS))
            weights = [(w_t, _layer_spec(w_t.shape, j, rows=qkv)), (w_rest_t, _const_spec(w_rest_t.shape))]
            proj, flog = _inproj(x2, g_pre_mix[i], weights, epi, n_flog=1, transposed=(0, 1))
            self_out = _fox(proj, flog, b_forget[j], batch, seq)
            mq_offset = 3 * npair
        mem_out = _memattn(proj, mq_offset, mem, g_mem[i], w_mem_kv, i, batch, seq)
        x2 = _tail(x2, self_out, mem_out, i, w_out_b, g_post_mix[i], g_pre_mlp[i], g_post_mlp[i], w_up_b, w_down_b)
    return x2.reshape(batch, seq, D_MODEL)
```

```python
import functools

import jax
import jax.numpy as jnp
from jax import lax
from jax.experimental import pallas as pl
from jax.experimental.pallas import tpu as pltpu

D_MODEL = 1024
MEM_HEADS = 4
MEM_HEAD_DIM = 64
MEM_WIDTH = MEM_HEADS * MEM_HEAD_DIM
RET_HEAD_DIM = 128
RET_HEADS = 6
RET_WIDTH = RET_HEADS * RET_HEAD_DIM
RET_CHUNK = 128
ROPE_BASE = 10000.0
FOX_HEAD_DIM = 64
FOX_HEADS = 12
FOX_WIDTH = FOX_HEADS * FOX_HEAD_DIM
D_FF = 4 * D_MODEL
EPS = 1e-6

LANES = 128
MXU_COLS = 256
VMEM_LIMIT_BYTES = 56 * 1024 * 1024
LOG2E = 1.4426950408889634
NEG_BIG = -1e30

ROW_TILE = 1024
ROW_GROUPS = 2
FF_CHUNK = 512
FOX_TILE = 256
CUM_TILE = 256
MEM_ROWS = 256

N_SELF_BLOCKS = RET_WIDTH // LANES
N_MEM_BLOCKS = MEM_WIDTH // LANES

_NT = (((1,), (1,)), ((), ()))
_F32 = jnp.float32
_BF16 = jnp.bfloat16


def _rms_scale(x):
    return lax.rsqrt(jnp.mean(x * x, axis=-1, keepdims=True) + EPS)


def _const_spec(shape):
    nd = len(shape)
    return pl.BlockSpec(shape, lambda *_: (0,) * nd, pipeline_mode=pl.Buffered(1))


ROPE_PACK = ROW_TILE // ROW_GROUPS // 2


def _rope_tables(pos_ref, inv_ref, part):
    ang = pos_ref[pl.ds(part * ROPE_PACK, ROPE_PACK), :] * inv_ref[...]
    c, s = jnp.cos(ang), jnp.sin(ang)
    half = RET_HEAD_DIM // 2
    c_sw, s_sw = pltpu.roll(c, half, 1), pltpu.roll(s, half, 1)
    low = lax.broadcasted_iota(jnp.int32, (1, LANES), 1) < half
    return (jnp.concatenate([jnp.where(low, c, c_sw), jnp.where(low, c_sw, c)], axis=0),
            jnp.concatenate([jnp.where(low, -s, s_sw), jnp.where(low, -s_sw, s)], axis=0))


def _rope_inputs(positions):
    t = positions.size
    half = RET_HEAD_DIM // 2
    inv_freq = ROPE_BASE ** (-jnp.arange(0, RET_HEAD_DIM, 2, dtype=_F32) / RET_HEAD_DIM)
    pos2 = positions.reshape(t // (2 * ROPE_PACK), 2, ROPE_PACK).transpose(0, 2, 1).reshape(t // 2, 2)
    return jnp.repeat(pos2.astype(_F32), half, axis=1), jnp.tile(inv_freq, 2).reshape(1, LANES)


CAST_COLS = 512


def _cast_weights(w_refs, w_scr, transposed=()):
    @pl.when(pl.program_id(0) == 0)
    def _():
        base = 0
        for n, w_ref in enumerate(w_refs):
            w2 = w_ref.at[0] if len(w_ref.shape) == 3 else w_ref
            width = w2.shape[0] if n in transposed else w2.shape[1]
            for c0 in range(0, width, CAST_COLS):
                c1 = min(c0 + CAST_COLS, width)
                block = w2[c0:c1, :].T if n in transposed else w2[:, c0:c1]
                w_scr[:, base + c0:base + c1] = block.astype(w_scr.dtype)
            base += width


def _inproj_kernel(*refs, epilogues, has_rope, n_flog, n_weights, transposed):
    x_ref, g_ref = refs[:2]
    w_refs = refs[2:2 + n_weights]
    rest = list(refs[2 + n_weights:])
    pos_ref, inv_ref = (rest.pop(0), rest.pop(0)) if has_rope else (None, None)
    o_ref = rest.pop(0)
    flog_ref = rest.pop(0) if n_flog else None
    w_scr = rest.pop(0)
    _cast_weights(w_refs, w_scr, transposed)
    nb = len(epilogues)
    sub = ROW_TILE // ROW_GROUPS
    groups = [pl.ds(part * sub, sub) for part in range(ROW_GROUPS)]

    def pre_norm(rows):
        x = x_ref[rows, :]
        return (x * _rms_scale(x) * g_ref[...]).astype(_BF16)

    def finish(rows, j, a, rope):
        kind, scale = epilogues[j]
        if kind == "rot":
            a = a * rope[0] + pltpu.roll(a, RET_HEAD_DIM // 2, 1) * rope[1]
        if scale != 1.0:
            a = a * scale
        o_ref[j, rows, :] = a.astype(o_ref.dtype)

    hs = [pre_norm(rows) for rows in groups]
    ropes = [_rope_tables(pos_ref, inv_ref, part) if has_rope else None for part in range(ROW_GROUPS)]
    for rows, h, rope in zip(groups, hs, ropes):
        for jb in range(nb // 2):
            acc = jnp.dot(h, w_scr[:, jb * MXU_COLS:(jb + 1) * MXU_COLS], preferred_element_type=_F32)
            finish(rows, 2 * jb, acc[:, :LANES], rope)
            finish(rows, 2 * jb + 1, acc[:, LANES:], rope)
        if n_flog:
            base = nb * LANES
            flog_ref[rows, :] = jnp.dot(h, w_scr[:, base:base + LANES], preferred_element_type=_F32)


def _inproj(x2, g, weights, epilogues, rope=None, n_flog=0, transposed=()):
    t = x2.shape[0]
    nb = len(epilogues)
    ncols = sum(spec.block_shape[-2 if n in transposed else -1] for n, (_, spec) in enumerate(weights))
    assert ncols == (nb + n_flog) * LANES
    in_specs = [pl.BlockSpec((ROW_TILE, D_MODEL), lambda i: (i, 0)), _const_spec((1, D_MODEL))]
    in_specs += [spec for _, spec in weights]
    args = [x2, g.reshape(1, D_MODEL)] + [w for w, _ in weights]
    if rope is not None:
        in_specs += [pl.BlockSpec((ROW_TILE // 2, LANES), lambda i: (i, 0)), _const_spec((1, LANES))]
        args += list(rope)
    out_shape = [jax.ShapeDtypeStruct((nb, t, LANES), _BF16)]
    out_specs = [pl.BlockSpec((nb, ROW_TILE, LANES), lambda i: (0, i, 0))]
    if n_flog:
        out_shape.append(jax.ShapeDtypeStruct((t, LANES), _F32))
        out_specs.append(pl.BlockSpec((ROW_TILE, LANES), lambda i: (i, 0)))
    return pl.pallas_call(
        functools.partial(_inproj_kernel, epilogues=tuple(epilogues), has_rope=rope is not None, n_flog=n_flog,
                          n_weights=len(weights), transposed=tuple(transposed)),
        out_shape=out_shape,
        grid=(t // ROW_TILE,),
        in_specs=in_specs,
        out_specs=out_specs,
        scratch_shapes=[pltpu.VMEM((D_MODEL, ncols), _BF16)],
        compiler_params=pltpu.CompilerParams(dimension_semantics=("arbitrary",), vmem_limit_bytes=VMEM_LIMIT_BYTES),
        name="inproj_rope" if rope is not None else "inproj",
    )(*args)


def _layer_spec(shape, layer, rows=None):
    return pl.BlockSpec((1, shape[1] if rows is None else rows, shape[2]), lambda *_: (layer, 0, 0),
                        pipeline_mode=pl.Buffered(1))


def _retention_kernel(q_ref, k_ref, v_ref, gate_ref, dmask_ref, qd_ref, kd_ref, cd_ref, o_ref, *, n_chunks):
    c = RET_CHUNK
    dmask = dmask_ref[0]
    qd = qd_ref[0]
    kd = kd_ref[0]
    cd = cd_ref[0]

    inner, kv = [], []
    for n in range(n_chunks):
        rows = pl.ds(n * c, c)
        qc, kc, vc = q_ref[0, rows, :], k_ref[0, rows, :], v_ref[0, rows, :]
        inner.append((lax.dot_general(qc, kc, _NT, preferred_element_type=_F32) * dmask).astype(_BF16))
        k_scaled_t = (kc.astype(_F32) * kd).T.astype(_BF16)
        kv.append(jnp.dot(k_scaled_t, vc, preferred_element_type=_F32))

    state = jnp.zeros((RET_HEAD_DIM, RET_HEAD_DIM), _F32)
    for n in range(n_chunks):
        rows = pl.ds(n * c, c)
        q_scaled = (q_ref[0, rows, :].astype(_F32) * qd).astype(_BF16)
        lhs = jnp.concatenate([inner[n], q_scaled], axis=1)
        rhs = jnp.concatenate([v_ref[0, rows, :], state.astype(_BF16)], axis=0)
        out = jnp.dot(lhs, rhs, preferred_element_type=_F32)
        state = state * cd + kv[n]
        r = out * _rms_scale(out)
        g = gate_ref[0, rows, :].astype(_F32)
        o_ref[0, rows, :] = (r * (g * jax.nn.sigmoid(g))).astype(o_ref.dtype)


def _retention_tables():
    c = RET_CHUNK
    log_gamma = jnp.log(1.0 - 2.0 ** (-5.0 - jnp.arange(RET_HEADS, dtype=_F32)))
    idx = jnp.arange(c, dtype=_F32)
    rel = idx[:, None] - idx[None, :]
    dmask = jnp.where(rel[None] >= 0, jnp.exp(jnp.maximum(rel, 0.0)[None] * log_gamma[:, None, None]), 0.0)
    q_decay = jnp.exp((idx[None, :] + 1.0) * log_gamma[:, None])
    k_decay = jnp.exp((c - 1.0 - idx[None, :]) * log_gamma[:, None])
    chunk_decay = jnp.exp(c * log_gamma)
    full = (RET_HEADS, c, RET_HEAD_DIM)
    return (dmask,
            jnp.broadcast_to(q_decay[:, :, None], full),
            jnp.broadcast_to(k_decay[:, :, None], full),
            jnp.broadcast_to(chunk_decay[:, None, None], full))


def _retention(proj, batch, seq):
    nh = RET_HEADS
    p4 = proj.reshape(proj.shape[0], batch, seq, LANES)

    def slab(group):
        return pl.BlockSpec((nh, 1, seq, LANES), lambda b: (group, b, 0, 0))

    def table():
        return _const_spec((nh, RET_CHUNK, RET_HEAD_DIM))

    def body(q_ref, k_ref, v_ref, gate_ref, dm, qd, kd, cd, o_ref):
        for h in range(nh):
            one = pl.ds(h, 1)
            _retention_kernel(q_ref.at[h], k_ref.at[h], v_ref.at[h], gate_ref.at[h],
                              dm.at[one], qd.at[one], kd.at[one], cd.at[one], o_ref.at[h], n_chunks=seq // RET_CHUNK)

    out = pl.pallas_call(
        body,
        out_shape=jax.ShapeDtypeStruct((nh, batch, seq, LANES), _BF16),
        grid=(batch,),
        in_specs=[slab(0), slab(1), slab(2), slab(3), table(), table(), table(), table()],
        out_specs=pl.BlockSpec((nh, 1, seq, LANES), lambda b: (0, b, 0, 0)),
        compiler_params=pltpu.CompilerParams(dimension_semantics=("arbitrary",), vmem_limit_bytes=VMEM_LIMIT_BYTES),
        name="retention",
    )(p4, p4, p4, p4, *_retention_tables())
    return out.reshape(nh, batch * seq, LANES)


def _split3(x):
    hi = x.astype(_BF16)
    r1 = x - hi.astype(_F32)
    mid = r1.astype(_BF16)
    lo = (r1 - mid.astype(_F32)).astype(_BF16)
    return hi, mid, lo


C_PART_STRIDE = 16
ONES_LANE = LANES - 1


def _fox_kernel(q_ref, k_ref, v_ref, flog_ref, bias_ref, sel_ref, tri_ref, o_ref,
                cpack_ref, qa_ref, ka_ref, va_ref, *, seq, new_batch):
    lane = lax.broadcasted_iota(jnp.int32, (1, LANES), 1)
    low_half = lane < FOX_HEAD_DIM
    keep = (low_half, jnp.logical_not(low_half))
    sum_lane = (FOX_HEAD_DIM, 0)

    def build_cpack():
        tri = tri_ref[...]
        pad_rows = LANES - 3 * C_PART_STRIDE
        filler = jnp.where(lax.broadcasted_iota(jnp.int32, (pad_rows, CUM_TILE), 0) == pad_rows - 1, 1.0, 0.0)
        carry = jnp.zeros((C_PART_STRIDE, 1), _F32)
        for i in range(seq // CUM_TILE):
            rows = pl.ds(i * CUM_TILE, CUM_TILE)
            z = flog_ref[0, rows, :].T[:C_PART_STRIDE, :] + bias_ref[...]
            log_f = jnp.minimum(z, 0.0) - jnp.log1p(jnp.exp(-jnp.abs(z)))
            hi, mid, lo = _split3(log_f)
            cs = (jnp.dot(hi, tri, preferred_element_type=_F32)
                  + jnp.dot(mid, tri, preferred_element_type=_F32)
                  + jnp.dot(lo, tri, preferred_element_type=_F32)) + carry
            carry = cs[:, CUM_TILE - 1:CUM_TILE]
            parts = [part.astype(_F32) for part in _split3(cs * LOG2E)]
            packed_t = jnp.concatenate(parts + [filler], axis=0)
            cpack_ref[rows, :] = packed_t.T.astype(_BF16)

    pl.when(new_batch)(build_cpack)

    blk = 512
    for r in range(seq // blk):
        rows = pl.ds(r * blk, blk)
        extras = jnp.dot(cpack_ref[rows, :], sel_ref[0], preferred_element_type=_F32).astype(_BF16)
        for e in range(2):
            qa_ref[e, rows, :] = jnp.where(keep[e], q_ref[0, rows, :], extras[:, :LANES])
            ka_ref[e, rows, :] = jnp.where(keep[e], k_ref[0, rows, :], extras[:, LANES:])
            ones_col = jnp.where(lane == sum_lane[e], 1.0, 0.0).astype(_BF16)
            va_ref[e, rows, :] = jnp.where(keep[e], v_ref[0, rows, :], ones_col)

    t = FOX_TILE
    row_id = lax.broadcasted_iota(jnp.int32, (t, t), 0)
    col_id = lax.broadcasted_iota(jnp.int32, (t, t), 1)
    causal = col_id <= row_id

    for qi in range(seq // t):
        q_rows = pl.ds(qi * t, t)
        acc = []
        for e in range(2):
            qa = qa_ref[e, q_rows, :]
            m_run = a = None
            for kj in range(qi + 1):
                k_rows = pl.ds(kj * t, t)
                s = lax.dot_general(qa, ka_ref[e, k_rows, :], _NT, preferred_element_type=_F32)
                if kj == qi:
                    s = jnp.where(causal, s, NEG_BIG)
                m_new = jnp.max(s, axis=-1, keepdims=True)
                if kj > 0:
                    m_new = jnp.maximum(m_run, m_new)
                pv = jnp.dot(jnp.exp2(s - m_new).astype(_BF16), va_ref[e, k_rows, :], preferred_element_type=_F32)
                a = pv if kj == 0 else a * jnp.exp2(m_run - m_new) + pv
                m_run = m_new
            acc.append(a / a[:, sum_lane[e]:sum_lane[e] + 1])
        o_ref[0, q_rows, :] = jnp.where(low_half, acc[0], acc[1]).astype(o_ref.dtype)


def _fox_selectors():
    import numpy as np
    npair = FOX_HEADS // 2
    sel = np.zeros((npair, LANES, 2 * LANES), np.float32)
    for h in range(FOX_HEADS):
        p, e = divmod(h, 2)
        base = FOX_HEAD_DIM if e == 0 else 0
        qcol, kcol = base, LANES + base
        for part in range(3):
            sel[p, part * C_PART_STRIDE + h, qcol + part] = 1.0
            sel[p, ONES_LANE, kcol + part] = 1.0
            sel[p, ONES_LANE, qcol + 3 + part] = 1.0
            sel[p, part * C_PART_STRIDE + h, kcol + 3 + part] = -1.0
    return jnp.asarray(sel, _BF16)


def _fox(proj, flog, b_forget, batch, seq):
    npair = FOX_HEADS // 2
    p4 = proj.reshape(proj.shape[0], batch, seq, LANES)
    flog3 = flog.reshape(batch, seq, LANES)
    bias = jnp.zeros((C_PART_STRIDE,), _F32).at[:FOX_HEADS].set(b_forget.astype(_F32))
    bias = jnp.broadcast_to(bias[:, None], (C_PART_STRIDE, CUM_TILE))
    tri = jnp.triu(jnp.ones((CUM_TILE, CUM_TILE), _F32)).astype(_BF16)

    def slab(offset):
        return pl.BlockSpec((1, 1, seq, LANES), lambda b, p: (offset + p, b, 0, 0))

    def body(q_ref, k_ref, v_ref, flog_ref, bias_ref, sel_ref, tri_ref, o_ref, *scratch):
        _fox_kernel(q_ref.at[0], k_ref.at[0], v_ref.at[0], flog_ref, bias_ref, sel_ref, tri_ref,
                    o_ref.at[0], *scratch, seq=seq, new_batch=pl.program_id(1) == 0)

    out = pl.pallas_call(
        body,
        out_shape=jax.ShapeDtypeStruct((npair, batch, seq, LANES), _BF16),
        grid=(batch, npair),
        in_specs=[slab(0), slab(npair), slab(2 * npair),
                  pl.BlockSpec((1, seq, LANES), lambda b, p: (b, 0, 0)),
                  _const_spec((C_PART_STRIDE, CUM_TILE)),
                  pl.BlockSpec((1, LANES, 2 * LANES), lambda b, p: (p, 0, 0)),
                  _const_spec((CUM_TILE, CUM_TILE))],
        out_specs=pl.BlockSpec((1, 1, seq, LANES), lambda b, p: (p, b, 0, 0)),
        scratch_shapes=[pltpu.VMEM((seq, LANES), _BF16),
                        pltpu.VMEM((2, seq, LANES), _BF16),
                        pltpu.VMEM((2, seq, LANES), _BF16),
                        pltpu.VMEM((2, seq, LANES), _BF16)],
        compiler_params=pltpu.CompilerParams(dimension_semantics=("arbitrary", "arbitrary"),
                                             vmem_limit_bytes=VMEM_LIMIT_BYTES),
        name="fox_attention",
    )(p4, p4, p4, flog3, bias, _fox_selectors(), tri)
    return out.reshape(npair, batch * seq, LANES)


def _memattn_kernel(mem_ref, g_ref, w_ref, mq_ref, o_ref, w_scr, *, seq):
    _cast_weights([w_ref], w_scr)
    mem = mem_ref[0]
    kvn = (mem * _rms_scale(mem) * g_ref[...]).astype(_BF16)
    kv = jnp.dot(kvn, w_scr[...], preferred_element_type=_F32)
    lane = lax.broadcasted_iota(jnp.int32, (1, LANES), 1)
    low_half = lane < MEM_HEAD_DIM
    scale = MEM_HEAD_DIM ** -0.5
    for p in range(N_MEM_BLOCKS):
        mk = (kv[:, p * LANES:(p + 1) * LANES] * scale).astype(_BF16)
        mv = kv[:, MEM_WIDTH + p * LANES:MEM_WIDTH + (p + 1) * LANES].astype(_BF16)
        heads = []
        for e in range(2):
            keep = low_half if e == 0 else jnp.logical_not(low_half)
            sum_lane = MEM_HEAD_DIM if e == 0 else 0
            ones_col = jnp.where(lane == sum_lane, 1.0, 0.0).astype(_BF16)
            heads.append((jnp.where(keep, mk, jnp.zeros_like(mk)), jnp.where(keep, mv, ones_col), sum_lane))
        row_tiles = [pl.ds(i * MEM_ROWS, MEM_ROWS) for i in range(seq // MEM_ROWS)]
        scores = [[lax.dot_general(mq_ref[p, rows, :], mk_e, _NT, preferred_element_type=_F32)
                   for mk_e, _, _ in heads] for rows in row_tiles]
        probs = [[jnp.exp(s - jnp.max(s, axis=-1, keepdims=True)).astype(_BF16) for s in pair] for pair in scores]
        for rows, pair in zip(row_tiles, probs):
            outs = []
            for prob, (_, mv_e, sum_lane) in zip(pair, heads):
                out = jnp.dot(prob, mv_e, preferred_element_type=_F32)
                outs.append(out / out[:, sum_lane:sum_lane + 1])
            o_ref[p, rows, :] = jnp.where(low_half, outs[0], outs[1]).astype(o_ref.dtype)


def _memattn(proj, mq_offset, mem, g_mem, w_mem_kv, layer, batch, seq):
    p4 = proj.reshape(proj.shape[0], batch, seq, LANES)
    mem_len = mem.shape[1]

    def body(mem_ref, g_ref, w_ref, mq_ref, o_ref, w_scr):
        _memattn_kernel(mem_ref, g_ref, w_ref, mq_ref.at[:, 0], o_ref.at[:, 0], w_scr, seq=seq)

    out = pl.pallas_call(
        body,
        out_shape=jax.ShapeDtypeStruct((N_MEM_BLOCKS, batch, seq, LANES), _BF16),
        grid=(batch,),
        in_specs=[pl.BlockSpec((1, mem_len, D_MODEL), lambda b: (b, 0, 0)),
                  _const_spec((1, D_MODEL)),
                  _layer_spec(w_mem_kv.shape, layer),
                  pl.BlockSpec((N_MEM_BLOCKS, 1, seq, LANES), lambda b: (mq_offset // N_MEM_BLOCKS, b, 0, 0))],
        out_specs=pl.BlockSpec((N_MEM_BLOCKS, 1, seq, LANES), lambda b: (0, b, 0, 0)),
        scratch_shapes=[pltpu.VMEM((D_MODEL, 2 * MEM_WIDTH), _BF16)],
        compiler_params=pltpu.CompilerParams(dimension_semantics=("arbitrary",), vmem_limit_bytes=VMEM_LIMIT_BYTES),
        name="memory_attention",
    )(mem, g_mem.reshape(1, D_MODEL), w_mem_kv, p4)
    return out.reshape(N_MEM_BLOCKS, batch * seq, LANES)


def _tail_kernel(x_ref, so_ref, mo_ref, wo_ref, g1_ref, g2_ref, g3_ref, wup_ref, wdn_ref, o_ref):
    sub = ROW_TILE // ROW_GROUPS
    groups = [pl.ds(part * sub, sub) for part in range(ROW_GROUPS)]

    def out_proj(rows):
        mixed = jnp.concatenate([so_ref[j, rows, :] for j in range(N_SELF_BLOCKS)]
                                + [mo_ref[j, rows, :] for j in range(N_MEM_BLOCKS)], axis=1)
        return jnp.dot(mixed, wo_ref[0], preferred_element_type=_F32)

    def mid_norms(rows, y):
        x1 = x_ref[rows, :] + y * _rms_scale(y) * g1_ref[...]
        return x1, (x1 * _rms_scale(x1) * g2_ref[...]).astype(_BF16)

    def mlp(h):
        acc = jnp.zeros((sub, D_MODEL), _F32)
        for c in range(D_FF // FF_CHUNK):
            cols = slice(c * FF_CHUNK, (c + 1) * FF_CHUNK)
            u = jnp.maximum(jnp.dot(h, wup_ref[0, :, cols], preferred_element_type=_F32), 0.0)
            acc = acc + jnp.dot((u * u).astype(_BF16), wdn_ref[0, cols, :], preferred_element_type=_F32)
        return acc

    ys = [out_proj(rows) for rows in groups]
    mids = [mid_norms(rows, y) for rows, y in zip(groups, ys)]
    for rows, (x1, h) in zip(groups, mids):
        acc = mlp(h)
        o_ref[rows, :] = x1 + acc * _rms_scale(acc) * g3_ref[...]


def _tail(x2, self_out, mem_out, layer, w_out, g_post_mix, g_pre_mlp, g_post_mlp, w_up, w_down):
    t = x2.shape[0]
    row = lambda: pl.BlockSpec((ROW_TILE, D_MODEL), lambda i: (i, 0))
    gain = lambda g: g.reshape(1, D_MODEL)
    return pl.pallas_call(
        _tail_kernel,
        out_shape=jax.ShapeDtypeStruct((t, D_MODEL), _F32),
        grid=(t // ROW_TILE,),
        in_specs=[row(),
                  pl.BlockSpec((N_SELF_BLOCKS, ROW_TILE, LANES), lambda i: (0, i, 0)),
                  pl.BlockSpec((N_MEM_BLOCKS, ROW_TILE, LANES), lambda i: (0, i, 0)),
                  _layer_spec(w_out.shape, layer),
                  _const_spec((1, D_MODEL)), _const_spec((1, D_MODEL)), _const_spec((1, D_MODEL)),
                  _layer_spec(w_up.shape, layer), _layer_spec(w_down.shape, layer)],
        out_specs=row(),
        compiler_params=pltpu.CompilerParams(dimension_semantics=("arbitrary",), vmem_limit_bytes=VMEM_LIMIT_BYTES),
        name="outproj_mlp",
    )(x2, self_out, mem_out, w_out, gain(g_post_mix), gain(g_pre_mlp), gain(g_post_mlp), w_up, w_down)


def kernel(x, mem, positions, w_in_ret, w_in_fox, b_forget, w_mem_kv, w_out, w_up, w_down,
           g_pre_mix, g_post_mix, g_pre_mlp, g_post_mlp, g_mem):
    batch, seq, _ = x.shape
    depth = w_out.shape[0]
    assert (batch * seq) % ROW_TILE == 0 and seq % FOX_TILE == 0 and seq % RET_CHUNK == 0
    x2 = x.reshape(batch * seq, D_MODEL)
    rope = _rope_inputs(positions)
    w_out_b, w_up_b, w_down_b = w_out.astype(_BF16), w_up.astype(_BF16), w_down.astype(_BF16)
    for i in range(depth):
        j = i // 2
        if i % 2 == 0:
            epi = ([("rot", 1.0)] * RET_HEADS + [("rot", RET_HEAD_DIM ** -0.5)] * RET_HEADS
                   + [("plain", 1.0)] * (2 * RET_HEADS + N_MEM_BLOCKS))
            (proj,) = _inproj(x2, g_pre_mix[i], [(w_in_ret, _layer_spec(w_in_ret.shape, j))], epi, rope=rope)
            self_out = _retention(proj, batch, seq)
            mq_offset = 4 * RET_HEADS
        else:
            qkv = 3 * FOX_WIDTH
            w_t = jnp.swapaxes(w_in_fox, 1, 2)
            rest_t = w_t[j, qkv:, :]
            w_rest_t = jnp.concatenate([rest_t[FOX_HEADS:],
                                        jnp.pad(rest_t[:FOX_HEADS], ((0, LANES - FOX_HEADS), (0, 0)))], axis=0)
            npair = FOX_HEADS // 2
            epi = ([("plain", FOX_HEAD_DIM ** -0.5 * LOG2E)] * npair + [("plain", 1.0)] * (2 * npair + N_MEM_BLOCKS))
            weights = [(w_t, _layer_spec(w_t.shape, j, rows=qkv)), (w_rest_t, _const_spec(w_rest_t.shape))]
            proj, flog = _inproj(x2, g_pre_mix[i], weights, epi, n_flog=1, transposed=(0, 1))
            self_out = _fox(proj, flog, b_forget[j], batch, seq)
            mq_offset = 3 * npair
        mem_out = _memattn(proj, mq_offset, mem, g_mem[i], w_mem_kv, i, batch, seq)
        x2 = _tail(x2, self_out, mem_out, i, w_out_b, g_post_mix[i], g_pre_mlp[i], g_post_mlp[i], w_up_b, w_down_b)
    return x2.reshape(batch, seq, D_MODEL)
```

```python
import functools

import jax
import jax.numpy as jnp
from jax import lax
from jax.experimental import pallas as pl
from jax.experimental.pallas import tpu as pltpu

D_MODEL = 1024
MEM_HEADS = 4
MEM_HEAD_DIM = 64
MEM_WIDTH = MEM_HEADS * MEM_HEAD_DIM
RET_HEAD_DIM = 128
RET_HEADS = 6
RET_WIDTH = RET_HEADS * RET_HEAD_DIM
RET_CHUNK = 128
ROPE_BASE = 10000.0
FOX_HEAD_DIM = 64
FOX_HEADS = 12
FOX_WIDTH = FOX_HEADS * FOX_HEAD_DIM
D_FF = 4 * D_MODEL
EPS = 1e-6

LANES = 128
MXU_COLS = 256
VMEM_LIMIT_BYTES = 56 * 1024 * 1024
LOG2E = 1.4426950408889634
NEG_BIG = -1e30

ROW_TILE = 1024
ROW_GROUPS = 2
FF_CHUNK = 512
FOX_TILE = 256
CUM_TILE = 256
MEM_ROWS = 256

N_SELF_BLOCKS = RET_WIDTH // LANES
N_MEM_BLOCKS = MEM_WIDTH // LANES

_NT = (((1,), (1,)), ((), ()))
_F32 = jnp.float32
_BF16 = jnp.bfloat16


def _rms_scale(x):
    return lax.rsqrt(jnp.mean(x * x, axis=-1, keepdims=True) + EPS)


def _const_spec(shape):
    nd = len(shape)
    return pl.BlockSpec(shape, lambda *_: (0,) * nd, pipeline_mode=pl.Buffered(1))


ROPE_PACK = ROW_TILE // ROW_GROUPS // 2


def _rope_tables(pos_ref, inv_ref, part):
    ang = pos_ref[pl.ds(part * ROPE_PACK, ROPE_PACK), :] * inv_ref[...]
    c, s = jnp.cos(ang), jnp.sin(ang)
    half = RET_HEAD_DIM // 2
    c_sw, s_sw = pltpu.roll(c, half, 1), pltpu.roll(s, half, 1)
    low = lax.broadcasted_iota(jnp.int32, (1, LANES), 1) < half
    return (jnp.concatenate([jnp.where(low, c, c_sw), jnp.where(low, c_sw, c)], axis=0),
            jnp.concatenate([jnp.where(low, -s, s_sw), jnp.where(low, -s_sw, s)], axis=0))


def _rope_inputs(positions):
    t = positions.size
    half = RET_HEAD_DIM // 2
    inv_freq = ROPE_BASE ** (-jnp.arange(0, RET_HEAD_DIM, 2, dtype=_F32) / RET_HEAD_DIM)
    pos2 = positions.reshape(t // (2 * ROPE_PACK), 2, ROPE_PACK).transpose(0, 2, 1).reshape(t // 2, 2)
    return jnp.repeat(pos2.astype(_F32), half, axis=1), jnp.tile(inv_freq, 2).reshape(1, LANES)


CAST_COLS = 512


def _cast_weights(w_refs, w_scr, transposed=()):
    @pl.when(pl.program_id(0) == 0)
    def _():
        base = 0
        for n, w_ref in enumerate(w_refs):
            w2 = w_ref.at[0] if len(w_ref.shape) == 3 else w_ref
            width = w2.shape[0] if n in transposed else w2.shape[1]
            for c0 in range(0, width, CAST_COLS):
                c1 = min(c0 + CAST_COLS, width)
                block = w2[c0:c1, :].T if n in transposed else w2[:, c0:c1]
                w_scr[:, base + c0:base + c1] = block.astype(w_scr.dtype)
            base += width


def _inproj_kernel(*refs, epilogues, has_rope, n_flog, n_weights, transposed):
    x_ref, g_ref = refs[:2]
    w_refs = refs[2:2 + n_weights]
    rest = list(refs[2 + n_weights:])
    pos_ref, inv_ref = (rest.pop(0), rest.pop(0)) if has_rope else (None, None)
    o_ref = rest.pop(0)
    flog_ref = rest.pop(0) if n_flog else None
    w_scr = rest.pop(0)
    _cast_weights(w_refs, w_scr, transposed)
    nb = len(epilogues)
    sub = ROW_TILE // ROW_GROUPS
    groups = [pl.ds(part * sub, sub) for part in range(ROW_GROUPS)]

    def pre_norm(rows):
        x = x_ref[rows, :]
        return (x * _rms_scale(x) * g_ref[...]).astype(_BF16)

    def finish(rows, j, a, rope):
        kind, scale = epilogues[j]
        if kind == "rot":
            a = a * rope[0] + pltpu.roll(a, RET_HEAD_DIM // 2, 1) * rope[1]
        if scale != 1.0:
            a = a * scale
        o_ref[j, rows, :] = a.astype(o_ref.dtype)

    hs = [pre_norm(rows) for rows in groups]
    ropes = [_rope_tables(pos_ref, inv_ref, part) if has_rope else None for part in range(ROW_GROUPS)]
    for rows, h, rope in zip(groups, hs, ropes):
        for jb in range(nb // 2):
            acc = jnp.dot(h, w_scr[:, jb * MXU_COLS:(jb + 1) * MXU_COLS], preferred_element_type=_F32)
            finish(rows, 2 * jb, acc[:, :LANES], rope)
            finish(rows, 2 * jb + 1, acc[:, LANES:], rope)
        if n_flog:
            base = nb * LANES
            flog_ref[rows, :] = jnp.dot(h, w_scr[:, base:base + LANES], preferred_element_type=_F32)


def _inproj(x2, g, weights, epilogues, rope=None, n_flog=0, transposed=()):
    t = x2.shape[0]
    nb = len(epilogues)
    ncols = sum(spec.block_shape[-2 if n in transposed else -1] for n, (_, spec) in enumerate(weights))
    assert ncols == (nb + n_flog) * LANES
    in_specs = [pl.BlockSpec((ROW_TILE, D_MODEL), lambda i: (i, 0)), _const_spec((1, D_MODEL))]
    in_specs += [spec for _, spec in weights]
    args = [x2, g.reshape(1, D_MODEL)] + [w for w, _ in weights]
    if rope is not None:
        in_specs += [pl.BlockSpec((ROW_TILE // 2, LANES), lambda i: (i, 0)), _const_spec((1, LANES))]
        args += list(rope)
    out_shape = [jax.ShapeDtypeStruct((nb, t, LANES), _BF16)]
    out_specs = [pl.BlockSpec((nb, ROW_TILE, LANES), lambda i: (0, i, 0))]
    if n_flog:
        out_shape.append(jax.ShapeDtypeStruct((t, LANES), _F32))
        out_specs.append(pl.BlockSpec((ROW_TILE, LANES), lambda i: (i, 0)))
    return pl.pallas_call(
        functools.partial(_inproj_kernel, epilogues=tuple(epilogues), has_rope=rope is not None, n_flog=n_flog,
                          n_weights=len(weights), transposed=tuple(transposed)),
        out_shape=out_shape,
        grid=(t // ROW_TILE,),
        in_specs=in_specs,
        out_specs=out_specs,
        scratch_shapes=[pltpu.VMEM((D_MODEL, ncols), _BF16)],
        compiler_params=pltpu.CompilerParams(dimension_semantics=("arbitrary",), vmem_limit_bytes=VMEM_LIMIT_BYTES),
        name="inproj_rope" if rope is not None else "inproj",
    )(*args)


def _layer_spec(shape, layer, rows=None):
    return pl.BlockSpec((1, shape[1] if rows is None else rows, shape[2]), lambda *_: (layer, 0, 0),
                        pipeline_mode=pl.Buffered(1))


def _retention_kernel(q_ref, k_ref, v_ref, gate_ref, dmask_ref, qd_ref, kd_ref, cd_ref, o_ref, *, n_chunks):
    c = RET_CHUNK
    dmask = dmask_ref[0]
    qd = qd_ref[0]
    kd = kd_ref[0]
    cd = cd_ref[0]

    inner, kv = [], []
    for n in range(n_chunks):
        rows = pl.ds(n * c, c)
        qc, kc, vc = q_ref[0, rows, :], k_ref[0, rows, :], v_ref[0, rows, :]
        inner.append((lax.dot_general(qc, kc, _NT, preferred_element_type=_F32) * dmask).astype(_BF16))
        k_scaled_t = (kc.astype(_F32) * kd).T.astype(_BF16)
        kv.append(jnp.dot(k_scaled_t, vc, preferred_element_type=_F32))

    state = jnp.zeros((RET_HEAD_DIM, RET_HEAD_DIM), _F32)
    for n in range(n_chunks):
        rows = pl.ds(n * c, c)
        q_scaled = (q_ref[0, rows, :].astype(_F32) * qd).astype(_BF16)
        lhs = jnp.concatenate([inner[n], q_scaled], axis=1)
        rhs = jnp.concatenate([v_ref[0, rows, :], state.astype(_BF16)], axis=0)
        out = jnp.dot(lhs, rhs, preferred_element_type=_F32)
        state = state * cd + kv[n]
        r = out * _rms_scale(out)
        g = gate_ref[0, rows, :].astype(_F32)
        o_ref[0, rows, :] = (r * (g * jax.nn.sigmoid(g))).astype(o_ref.dtype)


def _retention_tables():
    c = RET_CHUNK
    log_gamma = jnp.log(1.0 - 2.0 ** (-5.0 - jnp.arange(RET_HEADS, dtype=_F32)))
    idx = jnp.arange(c, dtype=_F32)
    rel = idx[:, None] - idx[None, :]
    dmask = jnp.where(rel[None] >= 0, jnp.exp(jnp.maximum(rel, 0.0)[None] * log_gamma[:, None, None]), 0.0)
    q_decay = jnp.exp((idx[None, :] + 1.0) * log_gamma[:, None])
    k_decay = jnp.exp((c - 1.0 - idx[None, :]) * log_gamma[:, None])
    chunk_decay = jnp.exp(c * log_gamma)
    full = (RET_HEADS, c, RET_HEAD_DIM)
    return (dmask,
            jnp.broadcast_to(q_decay[:, :, None], full),
            jnp.broadcast_to(k_decay[:, :, None], full),
            jnp.broadcast_to(chunk_decay[:, None, None], full))


def _retention(proj, mem, g_mem, w_mem_kv, layer, batch, seq):
    nh = RET_HEADS
    p4 = proj.reshape(proj.shape[0], batch, seq, LANES)
    mem_len = mem.shape[1]

    def slab(group):
        return pl.BlockSpec((nh, 1, seq, LANES), lambda b: (group, b, 0, 0))

    def table():
        return _const_spec((nh, RET_CHUNK, RET_HEAD_DIM))

    def body(q_ref, k_ref, v_ref, gate_ref, dm, qd, kd, cd, mem_ref, g_ref, w_ref, mq_ref, o_ref, mo_ref, w_scr):
        _memattn_kernel(mem_ref, g_ref, w_ref, mq_ref.at[:, 0], mo_ref.at[:, 0], w_scr, seq=seq)
        for h in range(nh):
            one = pl.ds(h, 1)
            _retention_kernel(q_ref.at[h], k_ref.at[h], v_ref.at[h], gate_ref.at[h],
                              dm.at[one], qd.at[one], kd.at[one], cd.at[one], o_ref.at[h], n_chunks=seq // RET_CHUNK)

    out, mem_out = pl.pallas_call(
        body,
        out_shape=[jax.ShapeDtypeStruct((nh, batch, seq, LANES), _BF16),
                   jax.ShapeDtypeStruct((N_MEM_BLOCKS, batch, seq, LANES), _BF16)],
        grid=(batch,),
        in_specs=[slab(0), slab(1), slab(2), slab(3), table(), table(), table(), table(),
                  pl.BlockSpec((1, mem_len, D_MODEL), lambda b: (b, 0, 0)),
                  _const_spec((1, D_MODEL)),
                  _layer_spec(w_mem_kv.shape, layer),
                  pl.BlockSpec((N_MEM_BLOCKS, 1, seq, LANES), lambda b: (4 * nh // N_MEM_BLOCKS, b, 0, 0))],
        out_specs=[pl.BlockSpec((nh, 1, seq, LANES), lambda b: (0, b, 0, 0)),
                   pl.BlockSpec((N_MEM_BLOCKS, 1, seq, LANES), lambda b: (0, b, 0, 0))],
        scratch_shapes=[pltpu.VMEM((D_MODEL, 2 * MEM_WIDTH), _BF16)],
        compiler_params=pltpu.CompilerParams(dimension_semantics=("arbitrary",), vmem_limit_bytes=VMEM_LIMIT_BYTES),
        name="retention_memattn",
    )(p4, p4, p4, p4, *_retention_tables(), mem, g_mem.reshape(1, D_MODEL), w_mem_kv, p4)
    return out.reshape(nh, batch * seq, LANES), mem_out.reshape(N_MEM_BLOCKS, batch * seq, LANES)


def _split3(x):
    hi = x.astype(_BF16)
    r1 = x - hi.astype(_F32)
    mid = r1.astype(_BF16)
    lo = (r1 - mid.astype(_F32)).astype(_BF16)
    return hi, mid, lo


C_PART_STRIDE = 16
ONES_LANE = LANES - 1


def _fox_kernel(q_ref, k_ref, v_ref, flog_ref, bias_ref, sel_ref, tri_ref, o_ref,
                cpack_ref, qa_ref, ka_ref, va_ref, *, seq, new_batch):
    lane = lax.broadcasted_iota(jnp.int32, (1, LANES), 1)
    low_half = lane < FOX_HEAD_DIM
    keep = (low_half, jnp.logical_not(low_half))
    sum_lane = (FOX_HEAD_DIM, 0)

    def build_cpack():
        tri = tri_ref[...]
        pad_rows = LANES - 3 * C_PART_STRIDE
        filler = jnp.where(lax.broadcasted_iota(jnp.int32, (pad_rows, CUM_TILE), 0) == pad_rows - 1, 1.0, 0.0)
        carry = jnp.zeros((C_PART_STRIDE, 1), _F32)
        for i in range(seq // CUM_TILE):
            rows = pl.ds(i * CUM_TILE, CUM_TILE)
            z = flog_ref[0, rows, :].T[:C_PART_STRIDE, :] + bias_ref[...]
            log_f = jnp.minimum(z, 0.0) - jnp.log1p(jnp.exp(-jnp.abs(z)))
            hi, mid, lo = _split3(log_f)
            cs = (jnp.dot(hi, tri, preferred_element_type=_F32)
                  + jnp.dot(mid, tri, preferred_element_type=_F32)
                  + jnp.dot(lo, tri, preferred_element_type=_F32)) + carry
            carry = cs[:, CUM_TILE - 1:CUM_TILE]
            parts = [part.astype(_F32) for part in _split3(cs * LOG2E)]
            packed_t = jnp.concatenate(parts + [filler], axis=0)
            cpack_ref[rows, :] = packed_t.T.astype(_BF16)

    pl.when(new_batch)(build_cpack)

    blk = 512
    for r in range(seq // blk):
        rows = pl.ds(r * blk, blk)
        extras = jnp.dot(cpack_ref[rows, :], sel_ref[0], preferred_element_type=_F32).astype(_BF16)
        for e in range(2):
            qa_ref[e, rows, :] = jnp.where(keep[e], q_ref[0, rows, :], extras[:, :LANES])
            ka_ref[e, rows, :] = jnp.where(keep[e], k_ref[0, rows, :], extras[:, LANES:])
            ones_col = jnp.where(lane == sum_lane[e], 1.0, 0.0).astype(_BF16)
            va_ref[e, rows, :] = jnp.where(keep[e], v_ref[0, rows, :], ones_col)

    t = FOX_TILE
    row_id = lax.broadcasted_iota(jnp.int32, (t, t), 0)
    col_id = lax.broadcasted_iota(jnp.int32, (t, t), 1)
    causal = col_id <= row_id

    for qi in range(seq // t):
        q_rows = pl.ds(qi * t, t)
        acc = []
        for e in range(2):
            qa = qa_ref[e, q_rows, :]
            m_run = a = None
            for kj in range(qi + 1):
                k_rows = pl.ds(kj * t, t)
                s = lax.dot_general(qa, ka_ref[e, k_rows, :], _NT, preferred_element_type=_F32)
                if kj == qi:
                    s = jnp.where(causal, s, NEG_BIG)
                m_new = jnp.max(s, axis=-1, keepdims=True)
                if kj > 0:
                    m_new = jnp.maximum(m_run, m_new)
                pv = jnp.dot(jnp.exp2(s - m_new).astype(_BF16), va_ref[e, k_rows, :], preferred_element_type=_F32)
                a = pv if kj == 0 else a * jnp.exp2(m_run - m_new) + pv
                m_run = m_new
            acc.append(a / a[:, sum_lane[e]:sum_lane[e] + 1])
        o_ref[0, q_rows, :] = jnp.where(low_half, acc[0], acc[1]).astype(o_ref.dtype)


def _fox_selectors():
    import numpy as np
    npair = FOX_HEADS // 2
    sel = np.zeros((npair, LANES, 2 * LANES), np.float32)
    for h in range(FOX_HEADS):
        p, e = divmod(h, 2)
        base = FOX_HEAD_DIM if e == 0 else 0
        qcol, kcol = base, LANES + base
        for part in range(3):
            sel[p, part * C_PART_STRIDE + h, qcol + part] = 1.0
            sel[p, ONES_LANE, kcol + part] = 1.0
            sel[p, ONES_LANE, qcol + 3 + part] = 1.0
            sel[p, part * C_PART_STRIDE + h, kcol + 3 + part] = -1.0
    return jnp.asarray(sel, _BF16)


def _fox(proj, flog, b_forget, batch, seq):
    npair = FOX_HEADS // 2
    p4 = proj.reshape(proj.shape[0], batch, seq, LANES)
    flog3 = flog.reshape(batch, seq, LANES)
    bias = jnp.zeros((C_PART_STRIDE,), _F32).at[:FOX_HEADS].set(b_forget.astype(_F32))
    bias = jnp.broadcast_to(bias[:, None], (C_PART_STRIDE, CUM_TILE))
    tri = jnp.triu(jnp.ones((CUM_TILE, CUM_TILE), _F32)).astype(_BF16)

    def slab(offset):
        return pl.BlockSpec((1, 1, seq, LANES), lambda b, p: (offset + p, b, 0, 0))

    def body(q_ref, k_ref, v_ref, flog_ref, bias_ref, sel_ref, tri_ref, o_ref, *scratch):
        _fox_kernel(q_ref.at[0], k_ref.at[0], v_ref.at[0], flog_ref, bias_ref, sel_ref, tri_ref,
                    o_ref.at[0], *scratch, seq=seq, new_batch=pl.program_id(1) == 0)

    out = pl.pallas_call(
        body,
        out_shape=jax.ShapeDtypeStruct((npair, batch, seq, LANES), _BF16),
        grid=(batch, npair),
        in_specs=[slab(0), slab(npair), slab(2 * npair),
                  pl.BlockSpec((1, seq, LANES), lambda b, p: (b, 0, 0)),
                  _const_spec((C_PART_STRIDE, CUM_TILE)),
                  pl.BlockSpec((1, LANES, 2 * LANES), lambda b, p: (p, 0, 0)),
                  _const_spec((CUM_TILE, CUM_TILE))],
        out_specs=pl.BlockSpec((1, 1, seq, LANES), lambda b, p: (p, b, 0, 0)),
        scratch_shapes=[pltpu.VMEM((seq, LANES), _BF16),
                        pltpu.VMEM((2, seq, LANES), _BF16),
                        pltpu.VMEM((2, seq, LANES), _BF16),
                        pltpu.VMEM((2, seq, LANES), _BF16)],
        compiler_params=pltpu.CompilerParams(dimension_semantics=("arbitrary", "arbitrary"),
                                             vmem_limit_bytes=VMEM_LIMIT_BYTES),
        name="fox_attention",
    )(p4, p4, p4, flog3, bias, _fox_selectors(), tri)
    return out.reshape(npair, batch * seq, LANES)


def _memattn_kernel(mem_ref, g_ref, w_ref, mq_ref, o_ref, w_scr, *, seq):
    _cast_weights([w_ref], w_scr)
    mem = mem_ref[0]
    kvn = (mem * _rms_scale(mem) * g_ref[...]).astype(_BF16)
    kv = jnp.dot(kvn, w_scr[...], preferred_element_type=_F32)
    lane = lax.broadcasted_iota(jnp.int32, (1, LANES), 1)
    low_half = lane < MEM_HEAD_DIM
    scale = MEM_HEAD_DIM ** -0.5
    for p in range(N_MEM_BLOCKS):
        mk = (kv[:, p * LANES:(p + 1) * LANES] * scale).astype(_BF16)
        mv = kv[:, MEM_WIDTH + p * LANES:MEM_WIDTH + (p + 1) * LANES].astype(_BF16)
        heads = []
        for e in range(2):
            keep = low_half if e == 0 else jnp.logical_not(low_half)
            sum_lane = MEM_HEAD_DIM if e == 0 else 0
            ones_col = jnp.where(lane == sum_lane, 1.0, 0.0).astype(_BF16)
            heads.append((jnp.where(keep, mk, jnp.zeros_like(mk)), jnp.where(keep, mv, ones_col), sum_lane))
        row_tiles = [pl.ds(i * MEM_ROWS, MEM_ROWS) for i in range(seq // MEM_ROWS)]
        scores = [[lax.dot_general(mq_ref[p, rows, :], mk_e, _NT, preferred_element_type=_F32)
                   for mk_e, _, _ in heads] for rows in row_tiles]
        probs = [[jnp.exp(s - jnp.max(s, axis=-1, keepdims=True)).astype(_BF16) for s in pair] for pair in scores]
        for rows, pair in zip(row_tiles, probs):
            outs = []
            for prob, (_, mv_e, sum_lane) in zip(pair, heads):
                out = jnp.dot(prob, mv_e, preferred_element_type=_F32)
                outs.append(out / out[:, sum_lane:sum_lane + 1])
            o_ref[p, rows, :] = jnp.where(low_half, outs[0], outs[1]).astype(o_ref.dtype)


def _memattn(proj, mq_offset, mem, g_mem, w_mem_kv, layer, batch, seq):
    p4 = proj.reshape(proj.shape[0], batch, seq, LANES)
    mem_len = mem.shape[1]

    def body(mem_ref, g_ref, w_ref, mq_ref, o_ref, w_scr):
        _memattn_kernel(mem_ref, g_ref, w_ref, mq_ref.at[:, 0], o_ref.at[:, 0], w_scr, seq=seq)

    out = pl.pallas_call(
        body,
        out_shape=jax.ShapeDtypeStruct((N_MEM_BLOCKS, batch, seq, LANES), _BF16),
        grid=(batch,),
        in_specs=[pl.BlockSpec((1, mem_len, D_MODEL), lambda b: (b, 0, 0)),
                  _const_spec((1, D_MODEL)),
                  _layer_spec(w_mem_kv.shape, layer),
                  pl.BlockSpec((N_MEM_BLOCKS, 1, seq, LANES), lambda b: (mq_offset // N_MEM_BLOCKS, b, 0, 0))],
        out_specs=pl.BlockSpec((N_MEM_BLOCKS, 1, seq, LANES), lambda b: (0, b, 0, 0)),
        scratch_shapes=[pltpu.VMEM((D_MODEL, 2 * MEM_WIDTH), _BF16)],
        compiler_params=pltpu.CompilerParams(dimension_semantics=("arbitrary",), vmem_limit_bytes=VMEM_LIMIT_BYTES),
        name="memory_attention",
    )(mem, g_mem.reshape(1, D_MODEL), w_mem_kv, p4)
    return out.reshape(N_MEM_BLOCKS, batch * seq, LANES)


def _tail_kernel(x_ref, so_ref, mo_ref, wo_ref, g1_ref, g2_ref, g3_ref, wup_ref, wdn_ref, o_ref):
    sub = ROW_TILE // ROW_GROUPS
    groups = [pl.ds(part * sub, sub) for part in range(ROW_GROUPS)]

    def out_proj(rows):
        mixed = jnp.concatenate([so_ref[j, rows, :] for j in range(N_SELF_BLOCKS)]
                                + [mo_ref[j, rows, :] for j in range(N_MEM_BLOCKS)], axis=1)
        return jnp.dot(mixed, wo_ref[0], preferred_element_type=_F32)

    def mid_norms(rows, y):
        x1 = x_ref[rows, :] + y * _rms_scale(y) * g1_ref[...]
        return x1, (x1 * _rms_scale(x1) * g2_ref[...]).astype(_BF16)

    def mlp(h):
        acc = jnp.zeros((sub, D_MODEL), _F32)
        for c in range(D_FF // FF_CHUNK):
            cols = slice(c * FF_CHUNK, (c + 1) * FF_CHUNK)
            u = jnp.maximum(jnp.dot(h, wup_ref[0, :, cols], preferred_element_type=_F32), 0.0)
            acc = acc + jnp.dot((u * u).astype(_BF16), wdn_ref[0, cols, :], preferred_element_type=_F32)
        return acc

    ys = [out_proj(rows) for rows in groups]
    mids = [mid_norms(rows, y) for rows, y in zip(groups, ys)]
    for rows, (x1, h) in zip(groups, mids):
        acc = mlp(h)
        o_ref[rows, :] = x1 + acc * _rms_scale(acc) * g3_ref[...]


def _tail(x2, self_out, mem_out, layer, w_out, g_post_mix, g_pre_mlp, g_post_mlp, w_up, w_down):
    t = x2.shape[0]
    row = lambda: pl.BlockSpec((ROW_TILE, D_MODEL), lambda i: (i, 0))
    gain = lambda g: g.reshape(1, D_MODEL)
    return pl.pallas_call(
        _tail_kernel,
        out_shape=jax.ShapeDtypeStruct((t, D_MODEL), _F32),
        grid=(t // ROW_TILE,),
        in_specs=[row(),
                  pl.BlockSpec((N_SELF_BLOCKS, ROW_TILE, LANES), lambda i: (0, i, 0)),
                  pl.BlockSpec((N_MEM_BLOCKS, ROW_TILE, LANES), lambda i: (0, i, 0)),
                  _layer_spec(w_out.shape, layer),
                  _const_spec((1, D_MODEL)), _const_spec((1, D_MODEL)), _const_spec((1, D_MODEL)),
                  _layer_spec(w_up.shape, layer), _layer_spec(w_down.shape, layer)],
        out_specs=row(),
        compiler_params=pltpu.CompilerParams(dimension_semantics=("arbitrary",), vmem_limit_bytes=VMEM_LIMIT_BYTES),
        name="outproj_mlp",
    )(x2, self_out, mem_out, w_out, gain(g_post_mix), gain(g_pre_mlp), gain(g_post_mlp), w_up, w_down)


def kernel(x, mem, positions, w_in_ret, w_in_fox, b_forget, w_mem_kv, w_out, w_up, w_down,
           g_pre_mix, g_post_mix, g_pre_mlp, g_post_mlp, g_mem):
    batch, seq, _ = x.shape
    depth = w_out.shape[0]
    assert (batch * seq) % ROW_TILE == 0 and seq % FOX_TILE == 0 and seq % RET_CHUNK == 0
    x2 = x.reshape(batch * seq, D_MODEL)
    rope = _rope_inputs(positions)
    w_out_b, w_up_b, w_down_b = w_out.astype(_BF16), w_up.astype(_BF16), w_down.astype(_BF16)
    for i in range(depth):
        j = i // 2
        if i % 2 == 0:
            epi = ([("rot", 1.0)] * RET_HEADS + [("rot", RET_HEAD_DIM ** -0.5)] * RET_HEADS
                   + [("plain", 1.0)] * (2 * RET_HEADS + N_MEM_BLOCKS))
            (proj,) = _inproj(x2, g_pre_mix[i], [(w_in_ret, _layer_spec(w_in_ret.shape, j))], epi, rope=rope)
            self_out, mem_out = _retention(proj, mem, g_mem[i], w_mem_kv, i, batch, seq)
        else:
            qkv = 3 * FOX_WIDTH
            w_t = jnp.swapaxes(w_in_fox, 1, 2)
            rest_t = w_t[j, qkv:, :]
            w_rest_t = jnp.concatenate([rest_t[FOX_HEADS:],
                                        jnp.pad(rest_t[:FOX_HEADS], ((0, LANES - FOX_HEADS), (0, 0)))], axis=0)
            npair = FOX_HEADS // 2
            epi = ([("plain", FOX_HEAD_DIM ** -0.5 * LOG2E)] * npair + [("plain", 1.0)] * (2 * npair + N_MEM_BLOCKS))
            weights = [(w_t, _layer_spec(w_t.shape, j, rows=qkv)), (w_rest_t, _const_spec(w_rest_t.shape))]
            proj, flog = _inproj(x2, g_pre_mix[i], weights, epi, n_flog=1, transposed=(0, 1))
            self_out = _fox(proj, flog, b_forget[j], batch, seq)
            mem_out = _memattn(proj, 3 * npair, mem, g_mem[i], w_mem_kv, i, batch, seq)
        x2 = _tail(x2, self_out, mem_out, i, w_out_b, g_post_mix[i], g_pre_mlp[i], g_post_mlp[i], w_up_b, w_down_b)
    return x2.reshape(batch, seq, D_MODEL)
```
